```python
import jax, jax.numpy as jnp
from jax import lax
import numpy as np

D_MODEL = 1024
BATCH = 16
SEQ = 2048
DEPTH = 1

HEAD_DIM = 64
A_HEADS = 8
A_WIDTH = A_HEADS * HEAD_DIM
CHUNK = 128
B_HEADS = 8
B_KV_HEADS = 2
B_GROUP = B_HEADS // B_KV_HEADS
B_WIDTH = B_HEADS * HEAD_DIM
KV_WIDTH = B_KV_HEADS * HEAD_DIM
D_MIX = A_WIDTH + B_WIDTH
IN_WIDTH = 2 * A_WIDTH + B_WIDTH + 2 * KV_WIDTH
Q_BLOCK = 128
GRID_W = 64
ROPE_BASE = 10000.0
N_EXPERTS = 16
CAPACITY_FACTOR = 2
EXPERT_FF = 1024
EPS = 1e-6

kernel_name = "hybrid_gmlp_axialgqa_ecmoe_encoder"


def rms_norm(x, g):
    xf = x.astype(jnp.float32)
    y = xf * lax.rsqrt(jnp.mean(xf * xf, axis=-1, keepdims=True) + EPS)
    return (y * g.astype(jnp.float32)).astype(x.dtype)


def rope_1d(x, pos):
    half = x.shape[-1] // 2
    inv_freq = ROPE_BASE ** (-jnp.arange(half, dtype=jnp.float32) / half)
    ang = pos.astype(jnp.float32)[:, None] * inv_freq[None, :]
    cos = jnp.cos(ang)[:, None, :]
    sin = jnp.sin(ang)[:, None, :]
    xf = x.astype(jnp.float32)
    x1, x2 = xf[..., :half], xf[..., half:]
    return jnp.concatenate([x1 * cos - x2 * sin, x2 * cos + x1 * sin], axis=-1).astype(x.dtype)


def axial_rope(x, row_pos, col_pos):
    d_axis = x.shape[-1] // 2
    return jnp.concatenate([rope_1d(x[..., :d_axis], row_pos),
                            rope_1d(x[..., d_axis:], col_pos)], axis=-1)


def chunked_spatial_gating(u, v, g_v, w_s, b_s):
    B, S, _ = u.shape
    n_chunks = S // CHUNK
    u = jax.nn.gelu(u).reshape(B, n_chunks, CHUNK, A_HEADS, HEAD_DIM)
    v = rms_norm(jax.nn.gelu(v).reshape(B, S, A_HEADS, HEAD_DIM), g_v)
    v = v.reshape(B, n_chunks, CHUNK, A_HEADS, HEAD_DIM)
    gate = jnp.einsum('hpq,bnqhd->bnphd', w_s, v) + b_s.T[None, None, :, :, None]
    return (u * gate).reshape(B, S, A_WIDTH)


def grid_attention(q, k, v, g_q, g_k):
    B, S, _ = q.shape
    rows = S // GRID_W
    row_pos = jnp.repeat(jnp.arange(rows, dtype=jnp.int32), GRID_W)
    col_pos = jnp.tile(jnp.arange(GRID_W, dtype=jnp.int32), rows)
    q = rms_norm(q.reshape(B, S, B_HEADS, HEAD_DIM), g_q)
    k = rms_norm(k.reshape(B, S, B_KV_HEADS, HEAD_DIM), g_k)
    v = v.reshape(B, S, B_KV_HEADS, HEAD_DIM)
    q = axial_rope(q, row_pos, col_pos)
    k = axial_rope(k, row_pos, col_pos)
    n_blocks = S // Q_BLOCK
    q = q.reshape(B, n_blocks, Q_BLOCK, B_KV_HEADS, B_GROUP, HEAD_DIM).transpose(1, 0, 2, 3, 4, 5)
    scale = HEAD_DIM ** -0.5

    def attend_block(qb):
        s = jnp.einsum('bqkgd,bskd->bkgqs', qb, k, preferred_element_type=jnp.float32) * scale
        p = jax.nn.softmax(s, axis=-1)
        return jnp.einsum('bkgqs,bskd->bqkgd', p.astype(v.dtype), v)

    o = lax.map(attend_block, q)
    return o.transpose(1, 0, 2, 3, 4, 5).reshape(B, S, B_WIDTH)


def expert_choice_ffn(h, w_router, w_gate, w_up, w_down):
    B, S, D = h.shape
    cap = CAPACITY_FACTOR * S // N_EXPERTS
    logits = jnp.einsum('bsd,de->bse', h, w_router, preferred_element_type=jnp.float32)
    affinity = jax.nn.softmax(logits, axis=-1)
    g, idx = lax.top_k(affinity.transpose(0, 2, 1), cap)
    b_idx = jnp.arange(B)[:, None, None]
    xs = h[b_idx, idx]
    a = jnp.einsum('becd,edf->becf', xs, w_gate)
    b = jnp.einsum('becd,edf->becf', xs, w_up)
    y = jnp.einsum('becf,efd->becd', jax.nn.silu(a) * b, w_down)
    y = y * g[..., None].astype(y.dtype)
    return jnp.zeros_like(h).at[b_idx, idx].add(y)


def setup_inputs(seed: int = 0) -> dict:
    key = jax.random.key(seed)
    ks = jax.random.split(key, 20)
    f32 = jnp.float32
    n = lambda k, shape, s: jax.random.normal(k, shape, f32) * s
    gain = lambda k, shape: 1.0 + 0.01 * jax.random.normal(k, shape, f32)
    return {
        "x": jax.random.normal(ks[0], (BATCH, SEQ, D_MODEL), f32),
        "norm_mix_g": gain(ks[1], (DEPTH, D_MODEL)),
        "w_in": n(ks[2], (DEPTH, D_MODEL, IN_WIDTH), D_MODEL ** -0.5),
        "gmlp_v_norm_g": gain(ks[3], (DEPTH, A_HEADS, HEAD_DIM)),
        "gmlp_w_s": n(ks[4], (DEPTH, A_HEADS, CHUNK, CHUNK), CHUNK ** -0.5),
        "gmlp_b_s": 1.0 + 0.1 * jax.random.normal(ks[5], (DEPTH, A_HEADS, CHUNK), f32),
        "q_norm_g": gain(ks[6], (DEPTH, HEAD_DIM)),
        "k_norm_g": gain(ks[7], (DEPTH, HEAD_DIM)),
        "group_norm_a_g": gain(ks[8], (DEPTH, A_WIDTH)),
        "group_norm_b_g": gain(ks[9], (DEPTH, B_WIDTH)),
        "w_out": n(ks[10], (DEPTH, D_MIX, D_MODEL), D_MIX ** -0.5),
        "norm_ffn_g": gain(ks[11], (DEPTH, D_MODEL)),
        "w_router": n(ks[12], (DEPTH, D_MODEL, N_EXPERTS), D_MODEL ** -0.5),
        "w_gate": n(ks[13], (DEPTH, N_EXPERTS, D_MODEL, EXPERT_FF), D_MODEL ** -0.5),
        "w_up": n(ks[14], (DEPTH, N_EXPERTS, D_MODEL, EXPERT_FF), D_MODEL ** -0.5),
        "w_down": n(ks[15], (DEPTH, N_EXPERTS, EXPERT_FF, D_MODEL), EXPERT_FF ** -0.5),
        "final_norm_g": gain(ks[16], (D_MODEL,)),
    }


def reference(x, norm_mix_g, w_in, gmlp_v_norm_g, gmlp_w_s, gmlp_b_s, q_norm_g, k_norm_g,
              group_norm_a_g, group_norm_b_g, w_out, norm_ffn_g, w_router, w_gate, w_up,
              w_down, final_norm_g):
    o_u = 0
    o_v = o_u + A_WIDTH
    o_q = o_v + A_WIDTH
    o_k = o_q + B_WIDTH
    o_vv = o_k + KV_WIDTH
    for l in range(DEPTH):
        h = rms_norm(x, norm_mix_g[l])
        p = jnp.einsum('bsd,dn->bsn', h, w_in[l])
        a_out = chunked_spatial_gating(p[..., o_u:o_v], p[..., o_v:o_q],
                                       gmlp_v_norm_g[l], gmlp_w_s[l], gmlp_b_s[l])
        b_out = grid_attention(p[..., o_q:o_k], p[..., o_k:o_vv], p[..., o_vv:],
                               q_norm_g[l], k_norm_g[l])
        mixed = jnp.concatenate([rms_norm(a_out, group_norm_a_g[l]),
                                 rms_norm(b_out, group_norm_b_g[l])], axis=-1)
        x = x + jnp.einsum('bsm,md->bsd', mixed, w_out[l])
        h2 = rms_norm(x, norm_ffn_g[l])
        x = x + expert_choice_ffn(h2, w_router[l], w_gate[l], w_up[l], w_down[l])
    return rms_norm(x, final_norm_g)
```

```python
import functools

import numpy as np
import jax
import jax.numpy as jnp
from jax import lax
from jax.experimental import pallas as pl
from jax.experimental.pallas import tpu as pltpu

D_MODEL = 1024
BATCH = 16
SEQ = 2048
HEAD_DIM = 64
A_HEADS = 8
A_WIDTH = A_HEADS * HEAD_DIM
CHUNK = 128
B_HEADS = 8
B_KV_HEADS = 2
B_GROUP = B_HEADS // B_KV_HEADS
B_WIDTH = B_HEADS * HEAD_DIM
KV_WIDTH = B_KV_HEADS * HEAD_DIM
IN_WIDTH = 2 * A_WIDTH + B_WIDTH + 2 * KV_WIDTH
GRID_W = 64
ROPE_BASE = 10000.0
N_EXPERTS = 16
CAP = 2 * SEQ // N_EXPERTS
EXPERT_FF = 1024
EPS = 1e-6
N_TOK = BATCH * SEQ

LANES = 128
MXU_DIM = 256
VMEM_LIMIT = 56 * 1024 * 1024

TM_IN = 256
TQ = 256
TM_FFN = 512

F32 = jnp.float32
BF16 = jnp.bfloat16


def _dot(a, b):
    return jnp.dot(a, b, preferred_element_type=F32)


def _dot_nt(a, b):
    return lax.dot_general(a, b, (((1,), (1,)), ((), ())), preferred_element_type=F32)


def _gelu(x):
    c = np.float32(np.sqrt(2.0 / np.pi))
    return x * (0.5 * (1.0 + jnp.tanh(c * (x + 0.044715 * (x * x * x)))))


def _group_meansq(x, gmat):
    wc = gmat.shape[0]
    xx = (x * x).astype(BF16)
    parts = [_dot(xx[:, j * wc:(j + 1) * wc], gmat) for j in range(x.shape[1] // wc)]
    ss = parts[0] if len(parts) == 1 else jnp.concatenate(parts, axis=1)
    return ss * (1.0 / HEAD_DIM)


def _rope(xc, cos, sin_signed, first_half):
    partner = jnp.where(first_half, pltpu.roll(xc, LANES - 16, 1), pltpu.roll(xc, 16, 1))
    return xc * cos + partner * sin_signed


def _inproj_kernel(x_ref, gmix_ref, win_ref, gv_ref, wpair_ref, bfull_ref, gq_ref, gk_ref,
                   ga_ref, cos_ref, sin_ref, g256_ref, g128_ref,
                   an_ref, q_ref, k_ref, v_ref):
    x = x_ref[...]
    ms = jnp.mean(x * x, axis=-1, keepdims=True)
    h = (x * lax.rsqrt(ms + EPS) * gmix_ref[...]).astype(BF16)
    p = _dot(h, win_ref[...])

    u = _gelu(p[:, 0:A_WIDTH])
    vg = _gelu(p[:, A_WIDTH:2 * A_WIDTH])
    vn = vg * lax.rsqrt(_group_meansq(vg, g256_ref[...]) + EPS) * gv_ref[...]

    lane = lax.broadcasted_iota(jnp.int32, (CHUNK, LANES), 1)
    left = lane < HEAD_DIM
    rows = []
    for ci in range(TM_IN // CHUNK):
        r0 = ci * CHUNK
        cols = []
        for j in range(A_WIDTH // LANES):
            vj = vn[r0:r0 + CHUNK, j * LANES:(j + 1) * LANES].astype(BF16)
            zero = jnp.zeros_like(vj)
            rhs = jnp.concatenate([jnp.where(left, vj, zero), jnp.where(left, zero, vj)], axis=0)
            gate = _dot(wpair_ref[j], rhs) + bfull_ref[:, j * LANES:(j + 1) * LANES]
            cols.append(u[r0:r0 + CHUNK, j * LANES:(j + 1) * LANES] * gate)
        rows.append(jnp.concatenate(cols, axis=1))
    a = jnp.concatenate(rows, axis=0)
    a_ms = jnp.mean(a * a, axis=-1, keepdims=True)
    an_ref[...] = (a * lax.rsqrt(a_ms + EPS) * ga_ref[...]).astype(BF16)

    cos = cos_ref[...]
    sin_signed = sin_ref[...]
    lane_t = lax.broadcasted_iota(jnp.int32, (TM_IN, LANES), 1)
    first_half = (lane_t % 32) < 16

    o_q = 2 * A_WIDTH
    q = p[:, o_q:o_q + B_WIDTH]
    qn = q * lax.rsqrt(_group_meansq(q, g256_ref[...]) + EPS) * gq_ref[...]
    scale = np.float32(HEAD_DIM ** -0.5)
    for j in range(B_WIDTH // LANES):
        qj = _rope(qn[:, j * LANES:(j + 1) * LANES], cos, sin_signed, first_half)
        q_ref[:, j * LANES:(j + 1) * LANES] = (qj * scale).astype(BF16)

    o_k = o_q + B_WIDTH
    k = p[:, o_k:o_k + KV_WIDTH]
    kn = k * lax.rsqrt(_group_meansq(k, g128_ref[...]) + EPS) * gk_ref[...]
    k_ref[...] = _rope(kn, cos, sin_signed, first_half).astype(BF16)
    v_ref[...] = p[:, o_k + KV_WIDTH:o_k + 2 * KV_WIDTH].astype(BF16)


def _inproj(x2, gmix, win, gv, wpair, bfull, gq, gk, ga, cos, sin_signed, g256, g128):
    n_steps = N_TOK // TM_IN
    per_seq = SEQ // TM_IN
    full = lambda shape: pl.BlockSpec(shape, lambda i: (0,) * len(shape))
    return pl.pallas_call(
        _inproj_kernel,
        grid=(n_steps,),
        in_specs=[
            pl.BlockSpec((TM_IN, D_MODEL), lambda i: (i, 0)),
            full((1, D_MODEL)),
            full((D_MODEL, IN_WIDTH)),
            full((1, A_WIDTH)),
            full((A_WIDTH // LANES, CHUNK, 2 * CHUNK)),
            full((CHUNK, A_WIDTH)),
            full((1, B_WIDTH)),
            full((1, KV_WIDTH)),
            full((1, A_WIDTH)),
            pl.BlockSpec((TM_IN, LANES), lambda i: (i % per_seq, 0)),
            pl.BlockSpec((TM_IN, LANES), lambda i: (i % per_seq, 0)),
            full((MXU_DIM, MXU_DIM)),
            full((LANES, LANES)),
        ],
        out_specs=[
            pl.BlockSpec((TM_IN, A_WIDTH), lambda i: (i, 0)),
            pl.BlockSpec((TM_IN, B_WIDTH), lambda i: (i, 0)),
            pl.BlockSpec((TM_IN, KV_WIDTH), lambda i: (i, 0)),
            pl.BlockSpec((TM_IN, KV_WIDTH), lambda i: (i, 0)),
        ],
        out_shape=[
            jax.ShapeDtypeStruct((N_TOK, A_WIDTH), BF16),
            jax.ShapeDtypeStruct((N_TOK, B_WIDTH), BF16),
            jax.ShapeDtypeStruct((N_TOK, KV_WIDTH), BF16),
            jax.ShapeDtypeStruct((N_TOK, KV_WIDTH), BF16),
        ],
        compiler_params=pltpu.CompilerParams(
            dimension_semantics=("arbitrary",), vmem_limit_bytes=VMEM_LIMIT),
        name="inproj",
    )(x2, gmix, win, gv, wpair, bfull, gq, gk, ga, cos, sin_signed, g256, g128)


def _attn_kernel(q_ref, k_ref, v_ref, an_ref, x_ref, woa_ref, wob_ref, gb_ref, gffn_ref,
                 wrh_ref, wrl_ref, x1_ref, h2_ref, aff_ref, o_scr):
    for hh in range(B_HEADS):
        kv = hh // B_GROUP
        qh = q_ref[:, hh * HEAD_DIM:(hh + 1) * HEAD_DIM]
        kh = k_ref[:, kv * HEAD_DIM:(kv + 1) * HEAD_DIM]
        vh = v_ref[:, kv * HEAD_DIM:(kv + 1) * HEAD_DIM]
        s = _dot_nt(qh, kh)
        m = jnp.max(s, axis=-1, keepdims=True)
        pe = jnp.exp(s - m)
        l = jnp.sum(pe, axis=-1, keepdims=True)
        o = _dot(pe.astype(BF16), vh)
        o_scr[:, hh * HEAD_DIM:(hh + 1) * HEAD_DIM] = o * (1.0 / l)
    o = o_scr[...]
    o_ms = jnp.mean(o * o, axis=-1, keepdims=True)
    bn = (o * lax.rsqrt(o_ms + EPS) * gb_ref[...]).astype(BF16)
    x1 = x_ref[...] + _dot(an_ref[...], woa_ref[...]) + _dot(bn, wob_ref[...])
    x1_ref[...] = x1
    ms = jnp.mean(x1 * x1, axis=-1, keepdims=True)
    h2 = x1 * lax.rsqrt(ms + EPS) * gffn_ref[...]
    h2_ref[...] = h2
    h2_hi = h2.astype(BF16)
    h2_lo = (h2 - h2_hi.astype(F32)).astype(BF16)
    logits = _dot(h2_hi, wrh_ref[...]) + _dot(h2_lo, wrh_ref[...]) + _dot(h2_hi, wrl_ref[...])
    lm = jnp.max(logits, axis=-1, keepdims=True)
    le = jnp.exp(logits - lm)
    aff_ref[...] = le / jnp.sum(le, axis=-1, keepdims=True)


def _attn(q, k, v, an, x2, woa, wob, gb, gffn, wrh, wrl):
    nq = SEQ // TQ
    full = lambda shape: pl.BlockSpec(shape, lambda b, i: (0,) * len(shape))
    tok = lambda w: pl.BlockSpec((TQ, w), lambda b, i: (b * nq + i, 0))
    return pl.pallas_call(
        _attn_kernel,
        grid=(BATCH, nq),
        in_specs=[
            tok(B_WIDTH),
            pl.BlockSpec((SEQ, KV_WIDTH), lambda b, i: (b, 0)),
            pl.BlockSpec((SEQ, KV_WIDTH), lambda b, i: (b, 0)),
            tok(A_WIDTH),
            tok(D_MODEL),
            full((A_WIDTH, D_MODEL)),
            full((B_WIDTH, D_MODEL)),
            full((1, B_WIDTH)),
            full((1, D_MODEL)),
            full((D_MODEL, N_EXPERTS)),
            full((D_MODEL, N_EXPERTS)),
        ],
        out_specs=[tok(D_MODEL), tok(D_MODEL), tok(N_EXPERTS)],
        out_shape=[
            jax.ShapeDtypeStruct((N_TOK, D_MODEL), F32),
            jax.ShapeDtypeStruct((N_TOK, D_MODEL), F32),
            jax.ShapeDtypeStruct((N_TOK, N_EXPERTS), F32),
        ],
        scratch_shapes=[pltpu.VMEM((TQ, B_WIDTH), F32)],
        compiler_params=pltpu.CompilerParams(
            dimension_semantics=("arbitrary", "arbitrary"), vmem_limit_bytes=VMEM_LIMIT),
        name="attn",
    )(q, k, v, an, x2, woa, wob, gb, gffn, wrh, wrl)


def _prefix_lanes(m, tri):
    outs = []
    run = jnp.zeros((m.shape[0], 1), F32)
    for t in range(SEQ // LANES):
        pt = _dot(m[:, t * LANES:(t + 1) * LANES].astype(BF16), tri) + run
        outs.append(pt)
        run = pt[:, LANES - 1:LANES]
    return jnp.concatenate(outs, axis=1)


def _topk_kernel(aff_ref, idx_ref, g_ref, psel_scr, hi_scr, mid_scr, lo_scr):
    a = aff_ref[0]
    key = pltpu.bitcast(a, jnp.int32)

    def bit_step(i, t):
        cand = t | jnp.left_shift(jnp.int32(1), 30 - i)
        cnt = jnp.sum(jnp.where(key >= cand, 1.0, 0.0), axis=1, keepdims=True)
        return jnp.where(cnt >= CAP, cand, t)

    t = lax.fori_loop(0, 31, bit_step, jnp.zeros((N_EXPERTS, 1), jnp.int32))
    gt = jnp.where(key > t, 1.0, 0.0)
    eq = jnp.where(key == t, 1.0, 0.0)
    need = CAP - jnp.sum(gt, axis=1, keepdims=True)

    r = lax.broadcasted_iota(jnp.int32, (LANES, LANES), 0)
    c = lax.broadcasted_iota(jnp.int32, (LANES, LANES), 1)
    tri = jnp.where(r <= c, 1.0, 0.0).astype(BF16)
    sel = gt + eq * jnp.where(_prefix_lanes(eq, tri) <= need, 1.0, 0.0)
    psel_scr[...] = sel * _prefix_lanes(sel, tri)

    hi = a.astype(BF16)
    r1 = a - hi.astype(F32)
    mid = r1.astype(BF16)
    lo = (r1 - mid.astype(F32)).astype(BF16)
    hi_scr[...] = hi.astype(F32)
    mid_scr[...] = mid.astype(F32)
    lo_scr[...] = lo.astype(F32)

    pos = lax.broadcasted_iota(jnp.int32, (8, SEQ), 1)
    row = lax.broadcasted_iota(jnp.int32, (8, SEQ), 0)
    pos_hi = (pos >> 7).astype(F32)
    pos_lo = (pos & 127).astype(F32)
    slot = (lax.broadcasted_iota(jnp.int32, (CAP, 1), 0) + 1).astype(F32)

    def expert_step(e, carry):
        prow = psel_scr[pl.ds(e, 1), :]
        onehot = jnp.where(prow == slot, 1.0, 0.0).astype(BF16)
        lhs = jnp.where(row == 0, hi_scr[pl.ds(e, 1), :],
              jnp.where(row == 1, mid_scr[pl.ds(e, 1), :],
              jnp.where(row == 2, lo_scr[pl.ds(e, 1), :],
              jnp.where(row == 3, pos_hi,
              jnp.where(row == 4, pos_lo, 0.0))))).astype(BF16)
        res = _dot_nt(lhs, onehot)
        g_ref[0, pl.ds(e, 1), :] = res[0:1] + res[1:2] + res[2:3]
        idx_ref[0, pl.ds(e, 1), :] = (res[3:4] * 128.0 + res[4:5]).astype(jnp.int32)
        return carry

    lax.fori_loop(0, N_EXPERTS, expert_step, 0)


def _topk(aff_t):
    return pl.pallas_call(
        _topk_kernel,
        grid=(BATCH,),
        in_specs=[pl.BlockSpec((1, N_EXPERTS, SEQ), lambda b: (b, 0, 0))],
        out_specs=[pl.BlockSpec((1, N_EXPERTS, CAP), lambda b: (b, 0, 0)),
                   pl.BlockSpec((1, N_EXPERTS, CAP), lambda b: (b, 0, 0))],
        out_shape=[jax.ShapeDtypeStruct((BATCH, N_EXPERTS, CAP), jnp.int32),
                   jax.ShapeDtypeStruct((BATCH, N_EXPERTS, CAP), F32)],
        scratch_shapes=[pltpu.VMEM((N_EXPERTS, SEQ), F32)] * 4,
        compiler_params=pltpu.CompilerParams(
            dimension_semantics=("arbitrary",), vmem_limit_bytes=VMEM_LIMIT),
        name="topk",
    )(aff_t)


def _gather_kernel(idx_ref, h_ref, o_ref):
    def body(c, carry):
        i = idx_ref[0, 0, c]
        o_ref[0, pl.ds(c, 1), :] = h_ref[0, pl.ds(i, 1), :]
        return carry

    lax.fori_loop(0, CAP, body, 0, unroll=8)


def _gather(idx3, h2):
    return pl.pallas_call(
        _gather_kernel,
        grid=(BATCH, N_EXPERTS),
        in_specs=[
            pl.BlockSpec((1, 1, CAP), lambda b, e: (b * N_EXPERTS + e, 0, 0),
                         memory_space=pltpu.SMEM),
            pl.BlockSpec((1, SEQ, D_MODEL), lambda b, e: (b, 0, 0)),
        ],
        out_specs=pl.BlockSpec((1, CAP, D_MODEL), lambda b, e: (e, b, 0)),
        out_shape=jax.ShapeDtypeStruct((N_EXPERTS, BATCH * CAP, D_MODEL), F32),
        compiler_params=pltpu.CompilerParams(
            dimension_semantics=("arbitrary", "arbitrary"), vmem_limit_bytes=VMEM_LIMIT),
        name="gather",
    )(idx3, h2)


def _ffn_kernel(xs_ref, wg_ref, wu_ref, wd_ref, y_ref, wg_s, wu_s, wd_s):
    @pl.when(pl.program_id(1) == 0)
    def _():
        wg_s[...] = wg_ref[0].astype(BF16)
        wu_s[...] = wu_ref[0].astype(BF16)
        wd_s[...] = wd_ref[0].astype(BF16)

    xs = xs_ref[0].astype(BF16)
    a = _dot(xs, wg_s[...])
    b = _dot(xs, wu_s[...])
    hm = (a / (1.0 + jnp.exp(-a)) * b).astype(BF16)
    y_ref[0] = _dot(hm, wd_s[...])


def _ffn(xs, w_gate, w_up, w_down):
    rows = BATCH * CAP
    wspec = lambda k, n: pl.BlockSpec((1, k, n), lambda e, j: (e, 0, 0))
    return pl.pallas_call(
        _ffn_kernel,
        grid=(N_EXPERTS, rows // TM_FFN),
        in_specs=[
            pl.BlockSpec((1, TM_FFN, D_MODEL), lambda e, j: (e, j, 0)),
            wspec(D_MODEL, EXPERT_FF), wspec(D_MODEL, EXPERT_FF), wspec(EXPERT_FF, D_MODEL),
        ],
        out_specs=pl.BlockSpec((1, TM_FFN, D_MODEL), lambda e, j: (e, j, 0)),
        out_shape=jax.ShapeDtypeStruct((N_EXPERTS, rows, D_MODEL), F32),
        scratch_shapes=[pltpu.VMEM((D_MODEL, EXPERT_FF), BF16),
                        pltpu.VMEM((D_MODEL, EXPERT_FF), BF16),
                        pltpu.VMEM((EXPERT_FF, D_MODEL), BF16)],
        compiler_params=pltpu.CompilerParams(
            dimension_semantics=("arbitrary", "arbitrary"), vmem_limit_bytes=VMEM_LIMIT),
        name="ffn",
    )(xs, w_gate, w_up, w_down)


NORM_ROWS = 256


def _scatter_kernel(idx_ref, g_ref, y_ref, x1_ref, gfin_ref, o_ref):
    e = pl.program_id(1)

    @pl.when(e == 0)
    def _():
        o_ref[...] = x1_ref[...]

    def body(c, carry):
        i = idx_ref[0, 0, c]
        g = g_ref[0, 0, c]
        o_ref[0, pl.ds(i, 1), :] = o_ref[0, pl.ds(i, 1), :] + g * y_ref[0, pl.ds(c, 1), :]
        return carry

    lax.fori_loop(0, CAP, body, 0, unroll=4)

    @pl.when(e == N_EXPERTS - 1)
    def _():
        def norm_step(r, carry):
            r0 = pl.multiple_of(r * NORM_ROWS, NORM_ROWS)
            xx = o_ref[0, pl.ds(r0, NORM_ROWS), :]
            ms = jnp.mean(xx * xx, axis=-1, keepdims=True)
            o_ref[0, pl.ds(r0, NORM_ROWS), :] = xx * lax.rsqrt(ms + EPS) * gfin_ref[...]
            return carry

        lax.fori_loop(0, SEQ // NORM_ROWS, norm_step, 0)


def _scatter(idx3, g3, y, x1, gfin):
    smem = lambda: pl.BlockSpec((1, 1, CAP), lambda b, e: (b * N_EXPERTS + e, 0, 0),
                                memory_space=pltpu.SMEM)
    return pl.pallas_call(
        _scatter_kernel,
        grid=(BATCH, N_EXPERTS),
        in_specs=[
            smem(), smem(),
            pl.BlockSpec((1, CAP, D_MODEL), lambda b, e: (e, b, 0)),
            pl.BlockSpec((1, SEQ, D_MODEL), lambda b, e: (b, 0, 0)),
            pl.BlockSpec((1, D_MODEL), lambda b, e: (0, 0)),
        ],
        out_specs=pl.BlockSpec((1, SEQ, D_MODEL), lambda b, e: (b, 0, 0)),
        out_shape=jax.ShapeDtypeStruct((BATCH, SEQ, D_MODEL), F32),
        compiler_params=pltpu.CompilerParams(
            dimension_semantics=("arbitrary", "arbitrary"), vmem_limit_bytes=VMEM_LIMIT),
        name="scatter",
    )(idx3, g3, y, x1, gfin)


def _rope_tables():
    half = HEAD_DIM // 4
    inv_freq = (np.float32(ROPE_BASE) ** (-np.arange(half, dtype=np.float32) / np.float32(half)))
    t = np.arange(SEQ)
    row_ang = (t // GRID_W).astype(np.float32)[:, None] * inv_freq[None, :]
    col_ang = (t % GRID_W).astype(np.float32)[:, None] * inv_freq[None, :]
    ang = np.concatenate([row_ang, row_ang, col_ang, col_ang], axis=1).astype(np.float64)
    sign = np.concatenate([-np.ones(half), np.ones(half), -np.ones(half), np.ones(half)])
    cos = np.tile(np.cos(ang), (1, LANES // HEAD_DIM)).astype(np.float32)
    sin_signed = np.tile(np.sin(ang) * sign[None, :], (1, LANES // HEAD_DIM)).astype(np.float32)
    return cos, sin_signed


def _group_ones(width):
    g = np.arange(width) // HEAD_DIM
    return (g[:, None] == g[None, :]).astype(np.float32)


def kernel(x, norm_mix_g, w_in, gmlp_v_norm_g, gmlp_w_s, gmlp_b_s, q_norm_g, k_norm_g,
           group_norm_a_g, group_norm_b_g, w_out, norm_ffn_g, w_router, w_gate, w_up,
           w_down, final_norm_g):
    cos, sin_signed = _rope_tables()
    g256 = jnp.asarray(_group_ones(MXU_DIM), BF16)
    g128 = jnp.asarray(_group_ones(LANES), BF16)

    x2 = x.reshape(N_TOK, D_MODEL)
    w_s = gmlp_w_s[0].astype(BF16)
    wpair = jnp.concatenate([w_s[0::2], w_s[1::2]], axis=2)
    bfull = jnp.repeat(gmlp_b_s[0].T, HEAD_DIM, axis=1)
    an, q, k, v = _inproj(
        x2, norm_mix_g[0][None, :], w_in[0].astype(BF16),
        gmlp_v_norm_g[0].reshape(1, A_WIDTH), wpair, bfull,
        jnp.tile(q_norm_g[0], B_HEADS)[None, :], jnp.tile(k_norm_g[0], B_KV_HEADS)[None, :],
        group_norm_a_g[0][None, :], jnp.asarray(cos), jnp.asarray(sin_signed), g256, g128)

    wo = w_out[0].astype(BF16)
    wr = w_router[0]
    wr_hi = wr.astype(BF16)
    wr_lo = (wr - wr_hi.astype(F32)).astype(BF16)
    x1, h2, aff = _attn(q, k, v, an, x2, wo[:A_WIDTH], wo[A_WIDTH:],
                        group_norm_b_g[0][None, :], norm_ffn_g[0][None, :], wr_hi, wr_lo)

    aff_t = aff.reshape(BATCH, SEQ, N_EXPERTS).transpose(0, 2, 1)
    idx, gates = _topk(aff_t)
    idx3 = idx.reshape(BATCH * N_EXPERTS, 1, CAP)
    g3 = gates.reshape(BATCH * N_EXPERTS, 1, CAP)

    xs = _gather(idx3, h2.reshape(BATCH, SEQ, D_MODEL))
    y = _ffn(xs, w_gate[0], w_up[0], w_down[0])
    return _scatter(idx3, g3, y, x1.reshape(BATCH, SEQ, D_MODEL), final_norm_g[None, :])
```

```python
import numpy as np
import jax
import jax.numpy as jnp
from jax import lax
from jax.experimental import pallas as pl
from jax.experimental.pallas import tpu as pltpu

D_MODEL = 1024
BATCH = 16
SEQ = 2048
HEAD_DIM = 64
A_HEADS = 8
A_WIDTH = A_HEADS * HEAD_DIM
CHUNK = 128
B_HEADS = 8
B_KV_HEADS = 2
B_GROUP = B_HEADS // B_KV_HEADS
B_WIDTH = B_HEADS * HEAD_DIM
KV_WIDTH = B_KV_HEADS * HEAD_DIM
IN_WIDTH = 2 * A_WIDTH + B_WIDTH + 2 * KV_WIDTH
GRID_W = 64
ROPE_BASE = 10000.0
N_EXPERTS = 16
CAP = 2 * SEQ // N_EXPERTS
EXPERT_FF = 1024
EPS = 1e-6
N_TOK = BATCH * SEQ

LANES = 128
MXU_DIM = 256
VMEM_LIMIT = 56 * 1024 * 1024

TM_IN = 256
TQ = 256
FFN_B = 2
FFN_ROWS = FFN_B * CAP
SC_E = 4
SC_ROWS = 4
HALF = D_MODEL // 2

F32 = jnp.float32
BF16 = jnp.bfloat16
U32 = jnp.uint32
HI_MASK = np.uint32(0xFFFF0000)


def _dot(a, b):
    return jnp.dot(a, b, preferred_element_type=F32)


def _dot_nt(a, b):
    return lax.dot_general(a, b, (((1,), (1,)), ((), ())), preferred_element_type=F32)


def _gelu(x):
    c = np.float32(np.sqrt(2.0 / np.pi))
    return x * (0.5 * (1.0 + jnp.tanh(c * (x + 0.044715 * (x * x * x)))))


def _group_meansq(x, gmat):
    wc = gmat.shape[0]
    xx = (x * x).astype(BF16)
    parts = [_dot(xx[:, j * wc:(j + 1) * wc], gmat) for j in range(x.shape[1] // wc)]
    ss = parts[0] if len(parts) == 1 else jnp.concatenate(parts, axis=1)
    return ss * (1.0 / HEAD_DIM)


def _rope(xc, cos, sin_signed, first_half):
    partner = jnp.where(first_half, pltpu.roll(xc, LANES - 16, 1), pltpu.roll(xc, 16, 1))
    return xc * cos + partner * sin_signed


def _pack_bf16_pairs(x):
    w = x.shape[1] // 2
    bits = pltpu.bitcast(x.astype(BF16).astype(F32), U32)
    return (bits[:, w:] & HI_MASK) | (bits[:, :w] >> 16)


def _unpack_lo(p):
    return pltpu.bitcast(p << 16, F32)


def _unpack_hi(p):
    return pltpu.bitcast(p & HI_MASK, F32)


def _inproj_kernel(x_ref, gmix_ref, win_ref, gv_ref, wpair_ref, bfull_ref, gq_ref, gk_ref,
                   ga_ref, cos_ref, sin_ref, g256_ref, g128_ref,
                   an_ref, q_ref, k_ref, v_ref):
    x = x_ref[...]
    ms = jnp.mean(x * x, axis=-1, keepdims=True)
    h = (x * lax.rsqrt(ms + EPS) * gmix_ref[...]).astype(BF16)
    p = _dot(h, win_ref[...])

    u = _gelu(p[:, 0:A_WIDTH])
    vg = _gelu(p[:, A_WIDTH:2 * A_WIDTH])
    vn = vg * lax.rsqrt(_group_meansq(vg, g256_ref[...]) + EPS) * gv_ref[...]

    lane = lax.broadcasted_iota(jnp.int32, (CHUNK, LANES), 1)
    left = lane < HEAD_DIM
    rows = []
    for ci in range(TM_IN // CHUNK):
        r0 = ci * CHUNK
        cols = []
        for j in range(A_WIDTH // LANES):
            vj = vn[r0:r0 + CHUNK, j * LANES:(j + 1) * LANES].astype(BF16)
            zero = jnp.zeros_like(vj)
            rhs = jnp.concatenate([jnp.where(left, vj, zero), jnp.where(left, zero, vj)], axis=0)
            gate = _dot(wpair_ref[j], rhs) + bfull_ref[:, j * LANES:(j + 1) * LANES]
            cols.append(u[r0:r0 + CHUNK, j * LANES:(j + 1) * LANES] * gate)
        rows.append(jnp.concatenate(cols, axis=1))
    a = jnp.concatenate(rows, axis=0)
    a_ms = jnp.mean(a * a, axis=-1, keepdims=True)
    an_ref[...] = (a * lax.rsqrt(a_ms + EPS) * ga_ref[...]).astype(BF16)

    cos = cos_ref[...]
    sin_signed = sin_ref[...]
    lane_t = lax.broadcasted_iota(jnp.int32, (TM_IN, LANES), 1)
    first_half = (lane_t % 32) < 16
    left_t = lane_t < HEAD_DIM

    o_q = 2 * A_WIDTH
    q = p[:, o_q:o_q + B_WIDTH]
    qn = q * lax.rsqrt(_group_meansq(q, g256_ref[...]) + EPS) * gq_ref[...]
    scale = np.float32(HEAD_DIM ** -0.5 * np.log2(np.e))
    for j in range(B_WIDTH // LANES):
        qj = _rope(qn[:, j * LANES:(j + 1) * LANES], cos, sin_signed, first_half) * scale
        q_ref[0, 2 * j] = jnp.where(left_t, qj, 0.0).astype(BF16)
        q_ref[0, 2 * j + 1] = jnp.where(left_t, 0.0, qj).astype(BF16)

    o_k = o_q + B_WIDTH
    k = p[:, o_k:o_k + KV_WIDTH]
    kn = k * lax.rsqrt(_group_meansq(k, g128_ref[...]) + EPS) * gk_ref[...]
    kr = _rope(kn, cos, sin_signed, first_half)
    kr_sw = pltpu.roll(kr, HEAD_DIM, 1)
    k_ref[0, 0] = jnp.where(left_t, kr, kr_sw).astype(BF16)
    k_ref[0, 1] = jnp.where(left_t, kr_sw, kr).astype(BF16)
    v = p[:, o_k + KV_WIDTH:o_k + 2 * KV_WIDTH]
    v_sw = pltpu.roll(v, HEAD_DIM, 1)
    v_ref[0, 0] = jnp.where(left_t, v, v_sw).astype(BF16)
    v_ref[0, 1] = jnp.where(left_t, v_sw, v).astype(BF16)


def _inproj(x2, gmix, win, gv, wpair, bfull, gq, gk, ga, cos, sin_signed, g256, g128):
    n_steps = N_TOK // TM_IN
    per_seq = SEQ // TM_IN
    full = lambda shape: pl.BlockSpec(shape, lambda i: (0,) * len(shape))
    return pl.pallas_call(
        _inproj_kernel,
        grid=(n_steps,),
        in_specs=[
            pl.BlockSpec((TM_IN, D_MODEL), lambda i: (i, 0)),
            full((1, D_MODEL)),
            full((D_MODEL, IN_WIDTH)),
            full((1, A_WIDTH)),
            full((A_WIDTH // LANES, CHUNK, 2 * CHUNK)),
            full((CHUNK, A_WIDTH)),
            full((1, B_WIDTH)),
            full((1, KV_WIDTH)),
            full((1, A_WIDTH)),
            pl.BlockSpec((TM_IN, LANES), lambda i: (i % per_seq, 0)),
            pl.BlockSpec((TM_IN, LANES), lambda i: (i % per_seq, 0)),
            full((MXU_DIM, MXU_DIM)),
            full((LANES, LANES)),
        ],
        out_specs=[
            pl.BlockSpec((TM_IN, A_WIDTH), lambda i: (i, 0)),
            pl.BlockSpec((1, B_HEADS, TM_IN, LANES), lambda i: (i // per_seq, 0, i % per_seq, 0)),
            pl.BlockSpec((1, B_KV_HEADS, TM_IN, LANES),
                         lambda i: (i // per_seq, 0, i % per_seq, 0)),
            pl.BlockSpec((1, B_KV_HEADS, TM_IN, LANES),
                         lambda i: (i // per_seq, 0, i % per_seq, 0)),
        ],
        out_shape=[
            jax.ShapeDtypeStruct((N_TOK, A_WIDTH), BF16),
            jax.ShapeDtypeStruct((BATCH, B_HEADS, SEQ, LANES), BF16),
            jax.ShapeDtypeStruct((BATCH, B_KV_HEADS, SEQ, LANES), BF16),
            jax.ShapeDtypeStruct((BATCH, B_KV_HEADS, SEQ, LANES), BF16),
        ],
        compiler_params=pltpu.CompilerParams(
            dimension_semantics=("arbitrary",), vmem_limit_bytes=VMEM_LIMIT),
        name="inproj",
    )(x2, gmix, win, gv, wpair, bfull, gq, gk, ga, cos, sin_signed, g256, g128)


ATT_ROWS = 256
ATT_M = 2 * ATT_ROWS
ATT_PAIRS = B_HEADS // 2
ATT_UNITS = (SEQ // ATT_ROWS) * ATT_PAIRS


def _att_unit(u):
    rb = u // ATT_PAIRS
    pair = u % ATT_PAIRS
    return pl.multiple_of(rb * ATT_ROWS, ATT_ROWS), pair, pair // (B_GROUP // 2)


ATT_KT = MXU_DIM


def _att_step(ua, uc, q_ref, k_ref, v_ref, o_ref, s_w, m_w, s_r, m_r, p_w, p_r):
    r0a, pair_a, g_a = _att_unit(ua)
    r0c, pair_c, g_c = _att_unit(uc)
    lhs = jnp.concatenate([q_ref[0, 2 * pair_a, pl.ds(r0a, ATT_ROWS), :],
                           q_ref[0, 2 * pair_a + 1, pl.ds(r0a, ATT_ROWS), :]], axis=0)
    m_prev = m_r[...]
    ones = jnp.ones((ATT_KT, LANES), BF16)
    m_run = None
    acc = None
    for j in range(SEQ // ATT_KT):
        ks = slice(j * ATT_KT, (j + 1) * ATT_KT)
        s = _dot_nt(lhs, k_ref[0, g_a, ks, :])
        s_w[:, ks] = s
        for h in range(ATT_KT // LANES):
            sh = s[:, h * LANES:(h + 1) * LANES]
            m_run = sh if m_run is None else jnp.maximum(m_run, sh)
        for h in range(ATT_KT // LANES):
            sl = slice(j * ATT_KT + h * LANES, j * ATT_KT + (h + 1) * LANES)
            p_w[:, sl] = jnp.exp2(s_r[:, sl] - m_prev).astype(BF16)
        rhs = jnp.concatenate([v_ref[0, g_c, ks, :], ones], axis=1)
        part = _dot(p_r[:, ks], rhs)
        acc = part if acc is None else acc + part
    m_w[...] = jnp.broadcast_to(jnp.max(m_run, axis=-1, keepdims=True), (ATT_M, LANES))
    on = acc[:, :LANES] * (1.0 / acc[:, LANES:])
    left = lax.broadcasted_iota(jnp.int32, (ATT_ROWS, LANES), 1) < HEAD_DIM
    o_ref[0, pair_c, pl.ds(r0c, ATT_ROWS), :] = jnp.where(left, on[:ATT_ROWS], on[ATT_ROWS:])


def _attn_kernel(q_ref, k_ref, v_ref, o_ref, s0, s1, m0, m1, p0, p1):
    s1[...] = jnp.zeros_like(s1)
    m1[...] = jnp.zeros_like(m1)
    p0[...] = jnp.zeros_like(p0)
    last = ATT_UNITS - 1

    def body(i, carry):
        it = 2 * i
        _att_step(jnp.minimum(it, last), jnp.maximum(it - 2, 0), q_ref, k_ref, v_ref, o_ref,
                  s0, m0, s1, m1, p1, p0)

        @pl.when(i >= 0)
        def _():
            it = 2 * i + 1
            _att_step(jnp.minimum(it, last), jnp.clip(it - 2, 0, last), q_ref, k_ref, v_ref,
                      o_ref, s1, m1, s0, m0, p0, p1)

        return carry

    lax.fori_loop(0, (ATT_UNITS + 2) // 2, body, 0)


def _attn(q, k, v):
    return pl.pallas_call(
        _attn_kernel,
        grid=(BATCH,),
        in_specs=[
            pl.BlockSpec((1, B_HEADS, SEQ, LANES), lambda b: (b, 0, 0, 0)),
            pl.BlockSpec((1, B_KV_HEADS, SEQ, LANES), lambda b: (b, 0, 0, 0)),
            pl.BlockSpec((1, B_KV_HEADS, SEQ, LANES), lambda b: (b, 0, 0, 0)),
        ],
        out_specs=pl.BlockSpec((1, ATT_PAIRS, SEQ, LANES), lambda b: (b, 0, 0, 0)),
        out_shape=jax.ShapeDtypeStruct((BATCH, ATT_PAIRS, SEQ, LANES), F32),
        scratch_shapes=[pltpu.VMEM((ATT_M, SEQ), F32), pltpu.VMEM((ATT_M, SEQ), F32),
                        pltpu.VMEM((ATT_M, LANES), F32), pltpu.VMEM((ATT_M, LANES), F32),
                        pltpu.VMEM((ATT_M, SEQ), BF16), pltpu.VMEM((ATT_M, SEQ), BF16)],
        compiler_params=pltpu.CompilerParams(
            dimension_semantics=("arbitrary",), vmem_limit_bytes=VMEM_LIMIT),
        name="attn",
    )(q, k, v)


def _outproj_kernel(o_ref, an_ref, x_ref, woa_ref, wob_ref, gb_ref, gffn_ref, wrh_ref, wrl_ref,
                    x1_ref, h2p_ref, aff_ref):
    o = jnp.concatenate([o_ref[0, j] for j in range(ATT_PAIRS)], axis=1)
    o_ms = jnp.mean(o * o, axis=-1, keepdims=True)
    bn = (o * lax.rsqrt(o_ms + EPS) * gb_ref[...]).astype(BF16)
    x1 = x_ref[...] + _dot(an_ref[...], woa_ref[...]) + _dot(bn, wob_ref[...])
    x1_ref[...] = x1
    ms = jnp.mean(x1 * x1, axis=-1, keepdims=True)
    h2 = x1 * lax.rsqrt(ms + EPS) * gffn_ref[...]
    h2p_ref[...] = _pack_bf16_pairs(h2)
    h2_hi = h2.astype(BF16)
    h2_lo = (h2 - h2_hi.astype(F32)).astype(BF16)
    logits = _dot(h2_hi, wrh_ref[...]) + _dot(h2_lo, wrh_ref[...]) + _dot(h2_hi, wrl_ref[...])
    lm = jnp.max(logits, axis=-1, keepdims=True)
    le = jnp.exp(logits - lm)
    aff_ref[...] = le / jnp.sum(le, axis=-1, keepdims=True)


def _outproj(o, an, x2, woa, wob, gb, gffn, wrh, wrl):
    nq = SEQ // TQ
    full = lambda shape: pl.BlockSpec(shape, lambda b, i: (0,) * len(shape))
    tok = lambda w: pl.BlockSpec((TQ, w), lambda b, i: (b * nq + i, 0))
    return pl.pallas_call(
        _outproj_kernel,
        grid=(BATCH, nq),
        in_specs=[
            pl.BlockSpec((1, ATT_PAIRS, TQ, LANES), lambda b, i: (b, 0, i, 0)),
            tok(A_WIDTH),
            tok(D_MODEL),
            full((A_WIDTH, D_MODEL)),
            full((B_WIDTH, D_MODEL)),
            full((1, B_WIDTH)),
            full((1, D_MODEL)),
            full((D_MODEL, N_EXPERTS)),
            full((D_MODEL, N_EXPERTS)),
        ],
        out_specs=[tok(D_MODEL), tok(HALF), tok(N_EXPERTS)],
        out_shape=[
            jax.ShapeDtypeStruct((N_TOK, D_MODEL), F32),
            jax.ShapeDtypeStruct((N_TOK, HALF), U32),
            jax.ShapeDtypeStruct((N_TOK, N_EXPERTS), F32),
        ],
        compiler_params=pltpu.CompilerParams(
            dimension_semantics=("arbitrary", "arbitrary"), vmem_limit_bytes=VMEM_LIMIT),
        name="outproj",
    )(o, an, x2, woa, wob, gb, gffn, wrh, wrl)


def _prefix_lanes(m, tri):
    outs = []
    run = jnp.zeros((m.shape[0], 1), F32)
    for t in range(SEQ // LANES):
        pt = _dot(m[:, t * LANES:(t + 1) * LANES].astype(BF16), tri) + run
        outs.append(pt)
        run = pt[:, LANES - 1:LANES]
    return jnp.concatenate(outs, axis=1)


def _topk_kernel(aff_ref, idx_ref, g_ref, psel_scr, hi_scr, mid_scr, lo_scr):
    a = aff_ref[0]
    key = pltpu.bitcast(a, jnp.int32)

    def bit_step(i, t):
        cand = t | jnp.left_shift(jnp.int32(1), 30 - i)
        cnt = jnp.sum(jnp.where(key >= cand, 1.0, 0.0), axis=1, keepdims=True)
        return jnp.where(cnt >= CAP, cand, t)

    t = lax.fori_loop(0, 31, bit_step, jnp.zeros((N_EXPERTS, 1), jnp.int32))
    gt = jnp.where(key > t, 1.0, 0.0)
    eq = jnp.where(key == t, 1.0, 0.0)
    need = CAP - jnp.sum(gt, axis=1, keepdims=True)

    r = lax.broadcasted_iota(jnp.int32, (LANES, LANES), 0)
    c = lax.broadcasted_iota(jnp.int32, (LANES, LANES), 1)
    tri = jnp.where(r <= c, 1.0, 0.0).astype(BF16)
    sel = gt + eq * jnp.where(_prefix_lanes(eq, tri) <= need, 1.0, 0.0)
    psel_scr[...] = sel * _prefix_lanes(sel, tri)

    hi = a.astype(BF16)
    r1 = a - hi.astype(F32)
    mid = r1.astype(BF16)
    lo = (r1 - mid.astype(F32)).astype(BF16)
    hi_scr[...] = hi.astype(F32)
    mid_scr[...] = mid.astype(F32)
    lo_scr[...] = lo.astype(F32)

    pos = lax.broadcasted_iota(jnp.int32, (8, SEQ), 1)
    row = lax.broadcasted_iota(jnp.int32, (8, SEQ), 0)
    pos_hi = (pos >> 7).astype(F32)
    pos_lo = (pos & 127).astype(F32)
    slot = (lax.broadcasted_iota(jnp.int32, (CAP, 1), 0) + 1).astype(F32)

    def expert_step(e, carry):
        prow = psel_scr[pl.ds(e, 1), :]
        onehot = jnp.where(prow == slot, 1.0, 0.0).astype(BF16)
        lhs = jnp.where(row == 0, hi_scr[pl.ds(e, 1), :],
              jnp.where(row == 1, mid_scr[pl.ds(e, 1), :],
              jnp.where(row == 2, lo_scr[pl.ds(e, 1), :],
              jnp.where(row == 3, pos_hi,
              jnp.where(row == 4, pos_lo, 0.0))))).astype(BF16)
        res = _dot_nt(lhs, onehot)
        g_ref[0, pl.ds(e, 1), :] = res[0:1] + res[1:2] + res[2:3]
        idx_ref[0, pl.ds(e, 1), :] = (res[3:4] * 128.0 + res[4:5]).astype(jnp.int32)
        return carry

    lax.fori_loop(0, N_EXPERTS, expert_step, 0)


def _topk(aff_t):
    return pl.pallas_call(
        _topk_kernel,
        grid=(BATCH,),
        in_specs=[pl.BlockSpec((1, N_EXPERTS, SEQ), lambda b: (b, 0, 0))],
        out_specs=[pl.BlockSpec((1, N_EXPERTS, CAP), lambda b: (b, 0, 0)),
                   pl.BlockSpec((1, N_EXPERTS, CAP), lambda b: (b, 0, 0))],
        out_shape=[jax.ShapeDtypeStruct((BATCH, N_EXPERTS, CAP), jnp.int32),
                   jax.ShapeDtypeStruct((BATCH, N_EXPERTS, CAP), F32)],
        scratch_shapes=[pltpu.VMEM((N_EXPERTS, SEQ), F32)] * 4,
        compiler_params=pltpu.CompilerParams(
            dimension_semantics=("arbitrary",), vmem_limit_bytes=VMEM_LIMIT),
        name="topk",
    )(aff_t)


FFN_NB = BATCH // FFN_B
FFN_ITEMS = N_EXPERTS * FFN_NB


def _ffn_kernel(idx_ref, h_ref, wg_ref, wu_ref, wd_ref, y_ref, wg_s, wu_s, wd_s, xs_scr):
    t = pl.program_id(0)
    item = jnp.maximum(t - 1, 0)

    @pl.when(t == 0)
    def _():
        xs_scr[...] = jnp.zeros_like(xs_scr)

    @pl.when(item % FFN_NB == 0)
    def _():
        wg_s[...] = wg_ref[0].astype(BF16)
        wu_s[...] = wu_ref[0].astype(BF16)
        wd_s[...] = wd_ref[0].astype(BF16)

    xs = xs_scr[(t + 1) % 2]
    xs_lo = _unpack_lo(xs).astype(BF16)
    xs_hi = _unpack_hi(xs).astype(BF16)
    a = _dot(xs_lo, wg_s[0:HALF, :]) + _dot(xs_hi, wg_s[HALF:, :])
    b = _dot(xs_lo, wu_s[0:HALF, :]) + _dot(xs_hi, wu_s[HALF:, :])
    hm = (a / (1.0 + jnp.exp(-a)) * b).astype(BF16)
    y_ref[0] = _pack_bf16_pairs(_dot(hm, wd_s[...]))

    slot = t % 2
    for bb in range(FFN_B):
        for c in range(CAP):
            i = idx_ref[0, bb, c]
            xs_scr[slot, pl.ds(bb * CAP + c, 1), :] = h_ref[bb, pl.ds(i, 1), :]


def _ffn(idx_eb, h2p, w_gate, w_up, w_down):
    item_of = lambda t: jnp.maximum(t - 1, 0)
    wspec = lambda k, n: pl.BlockSpec((1, k, n), lambda t: (item_of(t) // FFN_NB, 0, 0))
    return pl.pallas_call(
        _ffn_kernel,
        grid=(FFN_ITEMS + 1,),
        in_specs=[
            pl.BlockSpec((1, FFN_B, CAP), lambda t: (jnp.minimum(t, FFN_ITEMS - 1), 0, 0),
                         memory_space=pltpu.SMEM),
            pl.BlockSpec((FFN_B, SEQ, HALF), lambda t: (t % FFN_NB, 0, 0)),
            wspec(D_MODEL, EXPERT_FF), wspec(D_MODEL, EXPERT_FF), wspec(EXPERT_FF, D_MODEL),
        ],
        out_specs=pl.BlockSpec(
            (1, FFN_ROWS, HALF), lambda t: (item_of(t) // FFN_NB, item_of(t) % FFN_NB, 0)),
        out_shape=jax.ShapeDtypeStruct((N_EXPERTS, BATCH * CAP, HALF), U32),
        scratch_shapes=[pltpu.VMEM((D_MODEL, EXPERT_FF), BF16),
                        pltpu.VMEM((D_MODEL, EXPERT_FF), BF16),
                        pltpu.VMEM((EXPERT_FF, D_MODEL), BF16),
                        pltpu.VMEM((2, FFN_ROWS, HALF), U32)],
        compiler_params=pltpu.CompilerParams(
            dimension_semantics=("arbitrary",), vmem_limit_bytes=VMEM_LIMIT),
        name="ffn",
    )(idx_eb, h2p, w_gate, w_up, w_down)


NORM_ROWS = 256


def _scatter_kernel(idx_ref, g_ref, y_ref, x1_ref, gfin_ref, o_ref):
    ec = pl.program_id(1)

    @pl.when(ec == 0)
    def _():
        o_ref[...] = x1_ref[...]

    for el in range(SC_E):
        def body(cg, carry):
            c0 = cg * SC_ROWS
            rows = [idx_ref[0, el, c0 + r] for r in range(SC_ROWS)]
            gates = [g_ref[0, el, c0 + r] for r in range(SC_ROWS)]
            new = []
            for r in range(SC_ROWS):
                yp = y_ref[el, pl.ds(c0 + r, 1), :]
                cur = o_ref[0, pl.ds(rows[r], 1), :]
                new.append((cur[:, :HALF] + gates[r] * _unpack_lo(yp),
                            cur[:, HALF:] + gates[r] * _unpack_hi(yp)))
            for r in range(SC_ROWS):
                o_ref[0, pl.ds(rows[r], 1), 0:HALF] = new[r][0]
                o_ref[0, pl.ds(rows[r], 1), HALF:D_MODEL] = new[r][1]
            return carry

        lax.fori_loop(0, CAP // SC_ROWS, body, 0)

    @pl.when(ec == N_EXPERTS // SC_E - 1)
    def _():
        def norm_step(r, carry):
            r0 = pl.multiple_of(r * NORM_ROWS, NORM_ROWS)
            xx = o_ref[0, pl.ds(r0, NORM_ROWS), :]
            ms = jnp.mean(xx * xx, axis=-1, keepdims=True)
            o_ref[0, pl.ds(r0, NORM_ROWS), :] = xx * lax.rsqrt(ms + EPS) * gfin_ref[...]
            return carry

        lax.fori_loop(0, SEQ // NORM_ROWS, norm_step, 0)


def _scatter(idx_be, g_be, y, x1, gfin):
    ne = N_EXPERTS // SC_E
    smem = lambda: pl.BlockSpec((1, SC_E, CAP), lambda b, ec: (b * ne + ec, 0, 0),
                                memory_space=pltpu.SMEM)
    return pl.pallas_call(
        _scatter_kernel,
        grid=(BATCH, ne),
        in_specs=[
            smem(), smem(),
            pl.BlockSpec((SC_E, CAP, HALF), lambda b, ec: (ec, b, 0)),
            pl.BlockSpec((1, SEQ, D_MODEL), lambda b, ec: (b, 0, 0)),
            pl.BlockSpec((1, D_MODEL), lambda b, ec: (0, 0)),
        ],
        out_specs=pl.BlockSpec((1, SEQ, D_MODEL), lambda b, ec: (b, 0, 0)),
        out_shape=jax.ShapeDtypeStruct((BATCH, SEQ, D_MODEL), F32),
        compiler_params=pltpu.CompilerParams(
            dimension_semantics=("arbitrary", "arbitrary"), vmem_limit_bytes=VMEM_LIMIT),
        name="scatter",
    )(idx_be, g_be, y, x1, gfin)


def _rope_tables():
    half = HEAD_DIM // 4
    inv_freq = (np.float32(ROPE_BASE) ** (-np.arange(half, dtype=np.float32) / np.float32(half)))
    t = np.arange(SEQ)
    row_ang = (t // GRID_W).astype(np.float32)[:, None] * inv_freq[None, :]
    col_ang = (t % GRID_W).astype(np.float32)[:, None] * inv_freq[None, :]
    ang = np.concatenate([row_ang, row_ang, col_ang, col_ang], axis=1).astype(np.float64)
    sign = np.concatenate([-np.ones(half), np.ones(half), -np.ones(half), np.ones(half)])
    cos = np.tile(np.cos(ang), (1, LANES // HEAD_DIM)).astype(np.float32)
    sin_signed = np.tile(np.sin(ang) * sign[None, :], (1, LANES // HEAD_DIM)).astype(np.float32)
    return cos, sin_signed


def _group_ones(width):
    g = np.arange(width) // HEAD_DIM
    return (g[:, None] == g[None, :]).astype(np.float32)


def kernel(x, norm_mix_g, w_in, gmlp_v_norm_g, gmlp_w_s, gmlp_b_s, q_norm_g, k_norm_g,
           group_norm_a_g, group_norm_b_g, w_out, norm_ffn_g, w_router, w_gate, w_up,
           w_down, final_norm_g):
    cos, sin_signed = _rope_tables()
    g256 = jnp.asarray(_group_ones(MXU_DIM), BF16)
    g128 = jnp.asarray(_group_ones(LANES), BF16)

    x2 = x.reshape(N_TOK, D_MODEL)
    w_s = gmlp_w_s[0].astype(BF16)
    wpair = jnp.concatenate([w_s[0::2], w_s[1::2]], axis=2)
    bfull = jnp.repeat(gmlp_b_s[0].T, HEAD_DIM, axis=1)
    an, q, k, v = _inproj(
        x2, norm_mix_g[0][None, :], w_in[0].astype(BF16),
        gmlp_v_norm_g[0].reshape(1, A_WIDTH), wpair, bfull,
        jnp.tile(q_norm_g[0], B_HEADS)[None, :], jnp.tile(k_norm_g[0], B_KV_HEADS)[None, :],
        group_norm_a_g[0][None, :], jnp.asarray(cos), jnp.asarray(sin_signed), g256, g128)

    wo = w_out[0].astype(BF16)
    wr = w_router[0]
    wr_hi = wr.astype(BF16)
    wr_lo = (wr - wr_hi.astype(F32)).astype(BF16)
    o = _attn(q, k, v)
    x1, h2p, aff = _outproj(o, an, x2, wo[:A_WIDTH], wo[A_WIDTH:],
                            group_norm_b_g[0][None, :], norm_ffn_g[0][None, :], wr_hi, wr_lo)

    aff_t = aff.reshape(BATCH, SEQ, N_EXPERTS).transpose(0, 2, 1)
    idx, gates = _topk(aff_t)
    idx_eb = idx.transpose(1, 0, 2).reshape(FFN_ITEMS, FFN_B, CAP)
    idx_be = idx.reshape(BATCH * N_EXPERTS // SC_E, SC_E, CAP)
    g_be = gates.reshape(BATCH * N_EXPERTS // SC_E, SC_E, CAP)

    y = _ffn(idx_eb, h2p.reshape(BATCH, SEQ, HALF), w_gate[0], w_up[0], w_down[0])
    return _scatter(idx_be, g_be, y, x1.reshape(BATCH, SEQ, D_MODEL), final_norm_g[None, :])
```

```python
import numpy as np
import jax
import jax.numpy as jnp
from jax import lax
from jax.experimental import pallas as pl
from jax.experimental.pallas import tpu as pltpu

D_MODEL = 1024
BATCH = 16
SEQ = 2048
HEAD_DIM = 64
A_HEADS = 8
A_WIDTH = A_HEADS * HEAD_DIM
CHUNK = 128
B_HEADS = 8
B_KV_HEADS = 2
B_GROUP = B_HEADS // B_KV_HEADS
B_WIDTH = B_HEADS * HEAD_DIM
KV_WIDTH = B_KV_HEADS * HEAD_DIM
IN_WIDTH = 2 * A_WIDTH + B_WIDTH + 2 * KV_WIDTH
GRID_W = 64
ROPE_BASE = 10000.0
N_EXPERTS = 16
CAP = 2 * SEQ // N_EXPERTS
EXPERT_FF = 1024
EPS = 1e-6
N_TOK = BATCH * SEQ

LANES = 128
MXU_DIM = 256
VMEM_LIMIT = 56 * 1024 * 1024

TM_IN = 256
TQ = 256
FFN_B = 2
FFN_ROWS = FFN_B * CAP
SC_E = 4
SC_ROWS = 4
HALF = D_MODEL // 2

F32 = jnp.float32
BF16 = jnp.bfloat16
U32 = jnp.uint32
HI_MASK = np.uint32(0xFFFF0000)


def _dot(a, b):
    return jnp.dot(a, b, preferred_element_type=F32)


def _dot_nt(a, b):
    return lax.dot_general(a, b, (((1,), (1,)), ((), ())), preferred_element_type=F32)


def _gelu(x):
    c = np.float32(np.sqrt(2.0 / np.pi))
    return x * (0.5 * (1.0 + jnp.tanh(c * (x + 0.044715 * (x * x * x)))))


def _group_meansq(x, gmat):
    wc = gmat.shape[0]
    xx = (x * x).astype(BF16)
    parts = [_dot(xx[:, j * wc:(j + 1) * wc], gmat) for j in range(x.shape[1] // wc)]
    ss = parts[0] if len(parts) == 1 else jnp.concatenate(parts, axis=1)
    return ss * (1.0 / HEAD_DIM)


def _rope(xc, cos, sin_signed, first_half):
    partner = jnp.where(first_half, pltpu.roll(xc, LANES - 16, 1), pltpu.roll(xc, 16, 1))
    return xc * cos + partner * sin_signed


def _pack_bf16_pairs(x):
    w = x.shape[1] // 2
    bits = pltpu.bitcast(x.astype(BF16).astype(F32), U32)
    return (bits[:, w:] & HI_MASK) | (bits[:, :w] >> 16)


def _unpack_lo(p):
    return pltpu.bitcast(p << 16, F32)


def _unpack_hi(p):
    return pltpu.bitcast(p & HI_MASK, F32)


def _inproj_kernel(x_ref, gmix_ref, win_ref, gv_ref, wpair_ref, bfull_ref, gq_ref, gk_ref,
                   ga_ref, cos_ref, sin_ref, g256_ref, g128_ref,
                   an_ref, q_ref, k_ref, v_ref):
    x = x_ref[...]
    ms = jnp.mean(x * x, axis=-1, keepdims=True)
    h = (x * lax.rsqrt(ms + EPS) * gmix_ref[...]).astype(BF16)
    p = _dot(h, win_ref[...])

    u = _gelu(p[:, 0:A_WIDTH])
    vg = _gelu(p[:, A_WIDTH:2 * A_WIDTH])
    vn = vg * lax.rsqrt(_group_meansq(vg, g256_ref[...]) + EPS) * gv_ref[...]

    lane = lax.broadcasted_iota(jnp.int32, (CHUNK, LANES), 1)
    left = lane < HEAD_DIM
    rows = []
    for ci in range(TM_IN // CHUNK):
        r0 = ci * CHUNK
        cols = []
        for j in range(A_WIDTH // LANES):
            vj = vn[r0:r0 + CHUNK, j * LANES:(j + 1) * LANES].astype(BF16)
            zero = jnp.zeros_like(vj)
            rhs = jnp.concatenate([jnp.where(left, vj, zero), jnp.where(left, zero, vj)], axis=0)
            gate = _dot(wpair_ref[j], rhs) + bfull_ref[:, j * LANES:(j + 1) * LANES]
            cols.append(u[r0:r0 + CHUNK, j * LANES:(j + 1) * LANES] * gate)
        rows.append(jnp.concatenate(cols, axis=1))
    a = jnp.concatenate(rows, axis=0)
    a_ms = jnp.mean(a * a, axis=-1, keepdims=True)
    an_ref[...] = (a * lax.rsqrt(a_ms + EPS) * ga_ref[...]).astype(BF16)

    cos = cos_ref[...]
    sin_signed = sin_ref[...]
    lane_t = lax.broadcasted_iota(jnp.int32, (TM_IN, LANES), 1)
    first_half = (lane_t % 32) < 16
    left_t = lane_t < HEAD_DIM

    o_q = 2 * A_WIDTH
    q = p[:, o_q:o_q + B_WIDTH]
    qn = q * lax.rsqrt(_group_meansq(q, g256_ref[...]) + EPS) * gq_ref[...]
    scale = np.float32(HEAD_DIM ** -0.5 * np.log2(np.e))
    for j in range(B_WIDTH // LANES):
        qj = _rope(qn[:, j * LANES:(j + 1) * LANES], cos, sin_signed, first_half) * scale
        q_ref[0, 2 * j] = jnp.where(left_t, qj, 0.0).astype(BF16)
        q_ref[0, 2 * j + 1] = jnp.where(left_t, 0.0, qj).astype(BF16)

    o_k = o_q + B_WIDTH
    k = p[:, o_k:o_k + KV_WIDTH]
    kn = k * lax.rsqrt(_group_meansq(k, g128_ref[...]) + EPS) * gk_ref[...]
    kr = _rope(kn, cos, sin_signed, first_half)
    kr_sw = pltpu.roll(kr, HEAD_DIM, 1)
    k_ref[0, 0] = jnp.where(left_t, kr, kr_sw).astype(BF16)
    k_ref[0, 1] = jnp.where(left_t, kr_sw, kr).astype(BF16)
    v = p[:, o_k + KV_WIDTH:o_k + 2 * KV_WIDTH]
    v_sw = pltpu.roll(v, HEAD_DIM, 1)
    v_ref[0, 0] = jnp.where(left_t, v, v_sw).astype(BF16)
    v_ref[0, 1] = jnp.where(left_t, v_sw, v).astype(BF16)


def _inproj(x2, gmix, win, gv, wpair, bfull, gq, gk, ga, cos, sin_signed, g256, g128):
    n_steps = N_TOK // TM_IN
    per_seq = SEQ // TM_IN
    full = lambda shape: pl.BlockSpec(shape, lambda i: (0,) * len(shape))
    return pl.pallas_call(
        _inproj_kernel,
        grid=(n_steps,),
        in_specs=[
            pl.BlockSpec((TM_IN, D_MODEL), lambda i: (i, 0)),
            full((1, D_MODEL)),
            full((D_MODEL, IN_WIDTH)),
            full((1, A_WIDTH)),
            full((A_WIDTH // LANES, CHUNK, 2 * CHUNK)),
            full((CHUNK, A_WIDTH)),
            full((1, B_WIDTH)),
            full((1, KV_WIDTH)),
            full((1, A_WIDTH)),
            pl.BlockSpec((TM_IN, LANES), lambda i: (i % per_seq, 0)),
            pl.BlockSpec((TM_IN, LANES), lambda i: (i % per_seq, 0)),
            full((MXU_DIM, MXU_DIM)),
            full((LANES, LANES)),
        ],
        out_specs=[
            pl.BlockSpec((TM_IN, A_WIDTH), lambda i: (i, 0)),
            pl.BlockSpec((1, B_HEADS, TM_IN, LANES), lambda i: (i // per_seq, 0, i % per_seq, 0)),
            pl.BlockSpec((1, B_KV_HEADS, TM_IN, LANES),
                         lambda i: (i // per_seq, 0, i % per_seq, 0)),
            pl.BlockSpec((1, B_KV_HEADS, TM_IN, LANES),
                         lambda i: (i // per_seq, 0, i % per_seq, 0)),
        ],
        out_shape=[
            jax.ShapeDtypeStruct((N_TOK, A_WIDTH), BF16),
            jax.ShapeDtypeStruct((BATCH, B_HEADS, SEQ, LANES), BF16),
            jax.ShapeDtypeStruct((BATCH, B_KV_HEADS, SEQ, LANES), BF16),
            jax.ShapeDtypeStruct((BATCH, B_KV_HEADS, SEQ, LANES), BF16),
        ],
        compiler_params=pltpu.CompilerParams(
            dimension_semantics=("arbitrary",), vmem_limit_bytes=VMEM_LIMIT),
        name="inproj",
    )(x2, gmix, win, gv, wpair, bfull, gq, gk, ga, cos, sin_signed, g256, g128)


ATT_ROWS = 256
ATT_M = 2 * ATT_ROWS
ATT_PAIRS = B_HEADS // 2
ATT_UNITS = (SEQ // ATT_ROWS) * ATT_PAIRS


def _att_unit(u):
    rb = u // ATT_PAIRS
    pair = u % ATT_PAIRS
    return pl.multiple_of(rb * ATT_ROWS, ATT_ROWS), pair, pair // (B_GROUP // 2)


ATT_KT = MXU_DIM


def _att_step(ua, uc, q_ref, k_ref, v_ref, o_ref, s_w, m_w, s_r, m_r, p_w, p_r):
    r0a, pair_a, g_a = _att_unit(ua)
    r0c, pair_c, g_c = _att_unit(uc)
    lhs = jnp.concatenate([q_ref[0, 2 * pair_a, pl.ds(r0a, ATT_ROWS), :],
                           q_ref[0, 2 * pair_a + 1, pl.ds(r0a, ATT_ROWS), :]], axis=0)
    m_prev = m_r[...]
    ones = jnp.ones((ATT_KT, LANES), BF16)
    m_run = None
    acc = None
    for j in range(SEQ // ATT_KT):
        ks = slice(j * ATT_KT, (j + 1) * ATT_KT)
        s = _dot_nt(lhs, k_ref[0, g_a, ks, :])
        s_w[:, ks] = s
        for h in range(ATT_KT // LANES):
            sh = s[:, h * LANES:(h + 1) * LANES]
            m_run = sh if m_run is None else jnp.maximum(m_run, sh)
        for h in range(ATT_KT // LANES):
            sl = slice(j * ATT_KT + h * LANES, j * ATT_KT + (h + 1) * LANES)
            p_w[:, sl] = jnp.exp2(s_r[:, sl] - m_prev).astype(BF16)
        rhs = jnp.concatenate([v_ref[0, g_c, ks, :], ones], axis=1)
        part = _dot(p_r[:, ks], rhs)
        acc = part if acc is None else acc + part
    m_w[...] = jnp.broadcast_to(jnp.max(m_run, axis=-1, keepdims=True), (ATT_M, LANES))
    on = acc[:, :LANES] * (1.0 / acc[:, LANES:])
    left = lax.broadcasted_iota(jnp.int32, (ATT_ROWS, LANES), 1) < HEAD_DIM
    o_ref[0, pair_c, pl.ds(r0c, ATT_ROWS), :] = jnp.where(left, on[:ATT_ROWS], on[ATT_ROWS:])


def _attn_kernel(q_ref, k_ref, v_ref, o_ref, s0, s1, m0, m1, p0, p1):
    s1[...] = jnp.zeros_like(s1)
    m1[...] = jnp.zeros_like(m1)
    p0[...] = jnp.zeros_like(p0)
    last = ATT_UNITS - 1

    def body(i, carry):
        it = 2 * i
        _att_step(jnp.minimum(it, last), jnp.maximum(it - 2, 0), q_ref, k_ref, v_ref, o_ref,
                  s0, m0, s1, m1, p1, p0)

        @pl.when(i >= 0)
        def _():
            it = 2 * i + 1
            _att_step(jnp.minimum(it, last), jnp.clip(it - 2, 0, last), q_ref, k_ref, v_ref,
                      o_ref, s1, m1, s0, m0, p0, p1)

        return carry

    lax.fori_loop(0, (ATT_UNITS + 2) // 2, body, 0)


def _attn(q, k, v):
    return pl.pallas_call(
        _attn_kernel,
        grid=(BATCH,),
        in_specs=[
            pl.BlockSpec((1, B_HEADS, SEQ, LANES), lambda b: (b, 0, 0, 0)),
            pl.BlockSpec((1, B_KV_HEADS, SEQ, LANES), lambda b: (b, 0, 0, 0)),
            pl.BlockSpec((1, B_KV_HEADS, SEQ, LANES), lambda b: (b, 0, 0, 0)),
        ],
        out_specs=pl.BlockSpec((1, ATT_PAIRS, SEQ, LANES), lambda b: (b, 0, 0, 0)),
        out_shape=jax.ShapeDtypeStruct((BATCH, ATT_PAIRS, SEQ, LANES), F32),
        scratch_shapes=[pltpu.VMEM((ATT_M, SEQ), F32), pltpu.VMEM((ATT_M, SEQ), F32),
                        pltpu.VMEM((ATT_M, LANES), F32), pltpu.VMEM((ATT_M, LANES), F32),
                        pltpu.VMEM((ATT_M, SEQ), BF16), pltpu.VMEM((ATT_M, SEQ), BF16)],
        compiler_params=pltpu.CompilerParams(
            dimension_semantics=("arbitrary",), vmem_limit_bytes=VMEM_LIMIT),
        name="attn",
    )(q, k, v)


def _outproj_kernel(o_ref, an_ref, x_ref, woa_ref, wob_ref, gb_ref, gffn_ref, wrh_ref, wrl_ref,
                    x1_ref, h2p_ref, aff_ref):
    o = jnp.concatenate([o_ref[0, j] for j in range(ATT_PAIRS)], axis=1)
    o_ms = jnp.mean(o * o, axis=-1, keepdims=True)
    bn = (o * lax.rsqrt(o_ms + EPS) * gb_ref[...]).astype(BF16)
    x1 = x_ref[...] + _dot(an_ref[...], woa_ref[...]) + _dot(bn, wob_ref[...])
    x1_ref[...] = x1
    ms = jnp.mean(x1 * x1, axis=-1, keepdims=True)
    h2 = x1 * lax.rsqrt(ms + EPS) * gffn_ref[...]
    h2p_ref[...] = _pack_bf16_pairs(h2)
    h2_hi = h2.astype(BF16)
    h2_lo = (h2 - h2_hi.astype(F32)).astype(BF16)
    logits = _dot(h2_hi, wrh_ref[...]) + _dot(h2_lo, wrh_ref[...]) + _dot(h2_hi, wrl_ref[...])
    lm = jnp.max(logits, axis=-1, keepdims=True)
    le = jnp.exp(logits - lm)
    aff_ref[...] = le / jnp.sum(le, axis=-1, keepdims=True)


def _outproj(o, an, x2, woa, wob, gb, gffn, wrh, wrl):
    nq = SEQ // TQ
    full = lambda shape: pl.BlockSpec(shape, lambda b, i: (0,) * len(shape))
    tok = lambda w: pl.BlockSpec((TQ, w), lambda b, i: (b * nq + i, 0))
    return pl.pallas_call(
        _outproj_kernel,
        grid=(BATCH, nq),
        in_specs=[
            pl.BlockSpec((1, ATT_PAIRS, TQ, LANES), lambda b, i: (b, 0, i, 0)),
            tok(A_WIDTH),
            tok(D_MODEL),
            full((A_WIDTH, D_MODEL)),
            full((B_WIDTH, D_MODEL)),
            full((1, B_WIDTH)),
            full((1, D_MODEL)),
            full((D_MODEL, N_EXPERTS)),
            full((D_MODEL, N_EXPERTS)),
        ],
        out_specs=[tok(D_MODEL), tok(HALF), tok(N_EXPERTS)],
        out_shape=[
            jax.ShapeDtypeStruct((N_TOK, D_MODEL), F32),
            jax.ShapeDtypeStruct((N_TOK, HALF), U32),
            jax.ShapeDtypeStruct((N_TOK, N_EXPERTS), F32),
        ],
        compiler_params=pltpu.CompilerParams(
            dimension_semantics=("arbitrary", "arbitrary"), vmem_limit_bytes=VMEM_LIMIT),
        name="outproj",
    )(o, an, x2, woa, wob, gb, gffn, wrh, wrl)


def _prefix_lanes(m, tri):
    outs = []
    run = jnp.zeros((m.shape[0], 1), F32)
    for t in range(SEQ // LANES):
        pt = _dot(m[:, t * LANES:(t + 1) * LANES].astype(BF16), tri) + run
        outs.append(pt)
        run = pt[:, LANES - 1:LANES]
    return jnp.concatenate(outs, axis=1)


def _topk_kernel(aff_ref, idx_ref, g_ref, psel_scr, hi_scr, mid_scr, lo_scr):
    a = aff_ref[0]
    key = pltpu.bitcast(a, jnp.int32)

    def bit_step(i, t):
        cand = t | jnp.left_shift(jnp.int32(1), 30 - i)
        cnt = jnp.sum(jnp.where(key >= cand, 1.0, 0.0), axis=1, keepdims=True)
        return jnp.where(cnt >= CAP, cand, t)

    t = lax.fori_loop(0, 31, bit_step, jnp.zeros((N_EXPERTS, 1), jnp.int32))
    gt = jnp.where(key > t, 1.0, 0.0)
    eq = jnp.where(key == t, 1.0, 0.0)
    need = CAP - jnp.sum(gt, axis=1, keepdims=True)

    r = lax.broadcasted_iota(jnp.int32, (LANES, LANES), 0)
    c = lax.broadcasted_iota(jnp.int32, (LANES, LANES), 1)
    tri = jnp.where(r <= c, 1.0, 0.0).astype(BF16)
    sel = gt + eq * jnp.where(_prefix_lanes(eq, tri) <= need, 1.0, 0.0)
    psel_scr[...] = sel * _prefix_lanes(sel, tri)

    hi = a.astype(BF16)
    r1 = a - hi.astype(F32)
    mid = r1.astype(BF16)
    lo = (r1 - mid.astype(F32)).astype(BF16)
    hi_scr[...] = hi.astype(F32)
    mid_scr[...] = mid.astype(F32)
    lo_scr[...] = lo.astype(F32)

    pos = lax.broadcasted_iota(jnp.int32, (8, SEQ), 1)
    row = lax.broadcasted_iota(jnp.int32, (8, SEQ), 0)
    pos_hi = (pos >> 7).astype(F32)
    pos_lo = (pos & 127).astype(F32)
    slot = (lax.broadcasted_iota(jnp.int32, (CAP, 1), 0) + 1).astype(F32)

    def expert_step(e, carry):
        prow = psel_scr[pl.ds(e, 1), :]
        onehot = jnp.where(prow == slot, 1.0, 0.0).astype(BF16)
        lhs = jnp.where(row == 0, hi_scr[pl.ds(e, 1), :],
              jnp.where(row == 1, mid_scr[pl.ds(e, 1), :],
              jnp.where(row == 2, lo_scr[pl.ds(e, 1), :],
              jnp.where(row == 3, pos_hi,
              jnp.where(row == 4, pos_lo, 0.0))))).astype(BF16)
        res = _dot_nt(lhs, onehot)
        g_ref[0, pl.ds(e, 1), :] = res[0:1] + res[1:2] + res[2:3]
        idx_ref[0, pl.ds(e, 1), :] = (res[3:4] * 128.0 + res[4:5]).astype(jnp.int32)
        return carry

    lax.fori_loop(0, N_EXPERTS, expert_step, 0)


def _topk(aff_t):
    return pl.pallas_call(
        _topk_kernel,
        grid=(BATCH,),
        in_specs=[pl.BlockSpec((1, N_EXPERTS, SEQ), lambda b: (b, 0, 0))],
        out_specs=[pl.BlockSpec((1, N_EXPERTS, CAP), lambda b: (b, 0, 0)),
                   pl.BlockSpec((1, N_EXPERTS, CAP), lambda b: (b, 0, 0))],
        out_shape=[jax.ShapeDtypeStruct((BATCH, N_EXPERTS, CAP), jnp.int32),
                   jax.ShapeDtypeStruct((BATCH, N_EXPERTS, CAP), F32)],
        scratch_shapes=[pltpu.VMEM((N_EXPERTS, SEQ), F32)] * 4,
        compiler_params=pltpu.CompilerParams(
            dimension_semantics=("arbitrary",), vmem_limit_bytes=VMEM_LIMIT),
        name="topk",
    )(aff_t)


FFN_NB = BATCH // FFN_B
FFN_ITEMS = N_EXPERTS * FFN_NB


def _ffn_kernel(idx_ref, h_ref, wg_ref, wu_ref, wd_ref, y_ref, wg_s, wu_s, wd_s, xs_scr):
    t = pl.program_id(0)
    item = jnp.maximum(t - 1, 0)

    @pl.when(t == 0)
    def _():
        xs_scr[...] = jnp.zeros_like(xs_scr)

    @pl.when(item % FFN_NB == 0)
    def _():
        wg_s[...] = wg_ref[0].astype(BF16)
        wu_s[...] = wu_ref[0].astype(BF16)
        wd_s[...] = wd_ref[0].astype(BF16)

    xs = xs_scr[(t + 1) % 2]
    xs_lo = _unpack_lo(xs).astype(BF16)
    xs_hi = _unpack_hi(xs).astype(BF16)
    a = _dot(xs_lo, wg_s[0:HALF, :]) + _dot(xs_hi, wg_s[HALF:, :])
    b = _dot(xs_lo, wu_s[0:HALF, :]) + _dot(xs_hi, wu_s[HALF:, :])
    hm = (a / (1.0 + jnp.exp(-a)) * b).astype(BF16)
    y_ref[0] = _dot(hm, wd_s[...])

    slot = t % 2
    for bb in range(FFN_B):
        for c in range(CAP):
            i = idx_ref[0, bb, c]
            xs_scr[slot, pl.ds(bb * CAP + c, 1), :] = h_ref[bb, pl.ds(i, 1), :]


def _ffn(idx_eb, h2p, w_gate, w_up, w_down):
    item_of = lambda t: jnp.maximum(t - 1, 0)
    wspec = lambda k, n: pl.BlockSpec((1, k, n), lambda t: (item_of(t) // FFN_NB, 0, 0))
    return pl.pallas_call(
        _ffn_kernel,
        grid=(FFN_ITEMS + 1,),
        in_specs=[
            pl.BlockSpec((1, FFN_B, CAP), lambda t: (jnp.minimum(t, FFN_ITEMS - 1), 0, 0),
                         memory_space=pltpu.SMEM),
            pl.BlockSpec((FFN_B, SEQ, HALF), lambda t: (t % FFN_NB, 0, 0)),
            wspec(D_MODEL, EXPERT_FF), wspec(D_MODEL, EXPERT_FF), wspec(EXPERT_FF, D_MODEL),
        ],
        out_specs=pl.BlockSpec(
            (1, FFN_ROWS, D_MODEL), lambda t: (item_of(t) // FFN_NB, item_of(t) % FFN_NB, 0)),
        out_shape=jax.ShapeDtypeStruct((N_EXPERTS, BATCH * CAP, D_MODEL), F32),
        scratch_shapes=[pltpu.VMEM((D_MODEL, EXPERT_FF), BF16),
                        pltpu.VMEM((D_MODEL, EXPERT_FF), BF16),
                        pltpu.VMEM((EXPERT_FF, D_MODEL), BF16),
                        pltpu.VMEM((2, FFN_ROWS, HALF), U32)],
        compiler_params=pltpu.CompilerParams(
            dimension_semantics=("arbitrary",), vmem_limit_bytes=VMEM_LIMIT),
        name="ffn",
    )(idx_eb, h2p, w_gate, w_up, w_down)


NORM_ROWS = 256


def _scatter_kernel(idx_ref, g_ref, y_ref, x1_ref, gfin_ref, o_ref):
    ec = pl.program_id(1)

    @pl.when(ec == 0)
    def _():
        o_ref[...] = x1_ref[...]

    for el in range(SC_E):
        def load_group(c0):
            rows = [idx_ref[0, el, c0 + r] for r in range(SC_ROWS)]
            return rows, [o_ref[0, pl.ds(rows[r], 1), :] for r in range(SC_ROWS)]

        rows, cur = load_group(0)
        for c0 in range(0, CAP, SC_ROWS):
            nxt = load_group(c0 + SC_ROWS) if c0 + SC_ROWS < CAP else None
            for r in range(SC_ROWS):
                gate = g_ref[0, el, c0 + r]
                o_ref[0, pl.ds(rows[r], 1), :] = cur[r] + gate * y_ref[el, c0 + r:c0 + r + 1, :]
            if nxt is not None:
                rows, cur = nxt

    @pl.when(ec == N_EXPERTS // SC_E - 1)
    def _():
        def norm_step(r, carry):
            r0 = pl.multiple_of(r * NORM_ROWS, NORM_ROWS)
            xx = o_ref[0, pl.ds(r0, NORM_ROWS), :]
            ms = jnp.mean(xx * xx, axis=-1, keepdims=True)
            o_ref[0, pl.ds(r0, NORM_ROWS), :] = xx * lax.rsqrt(ms + EPS) * gfin_ref[...]
            return carry

        lax.fori_loop(0, SEQ // NORM_ROWS, norm_step, 0)


def _scatter(idx_be, g_be, y, x1, gfin):
    ne = N_EXPERTS // SC_E
    smem = lambda: pl.BlockSpec((1, SC_E, CAP), lambda b, ec: (b * ne + ec, 0, 0),
                                memory_space=pltpu.SMEM)
    return pl.pallas_call(
        _scatter_kernel,
        grid=(BATCH, ne),
        in_specs=[
            smem(), smem(),
            pl.BlockSpec((SC_E, CAP, D_MODEL), lambda b, ec: (ec, b, 0)),
            pl.BlockSpec((1, SEQ, D_MODEL), lambda b, ec: (b, 0, 0)),
            pl.BlockSpec((1, D_MODEL), lambda b, ec: (0, 0)),
        ],
        out_specs=pl.BlockSpec((1, SEQ, D_MODEL), lambda b, ec: (b, 0, 0)),
        out_shape=jax.ShapeDtypeStruct((BATCH, SEQ, D_MODEL), F32),
        compiler_params=pltpu.CompilerParams(
            dimension_semantics=("arbitrary", "arbitrary"), vmem_limit_bytes=VMEM_LIMIT),
        name="scatter",
    )(idx_be, g_be, y, x1, gfin)


def _rope_tables():
    half = HEAD_DIM // 4
    inv_freq = (np.float32(ROPE_BASE) ** (-np.arange(half, dtype=np.float32) / np.float32(half)))
    t = np.arange(SEQ)
    row_ang = (t // GRID_W).astype(np.float32)[:, None] * inv_freq[None, :]
    col_ang = (t % GRID_W).astype(np.float32)[:, None] * inv_freq[None, :]
    ang = np.concatenate([row_ang, row_ang, col_ang, col_ang], axis=1).astype(np.float64)
    sign = np.concatenate([-np.ones(half), np.ones(half), -np.ones(half), np.ones(half)])
    cos = np.tile(np.cos(ang), (1, LANES // HEAD_DIM)).astype(np.float32)
    sin_signed = np.tile(np.sin(ang) * sign[None, :], (1, LANES // HEAD_DIM)).astype(np.float32)
    return cos, sin_signed


def _group_ones(width):
    g = np.arange(width) // HEAD_DIM
    return (g[:, None] == g[None, :]).astype(np.float32)


def kernel(x, norm_mix_g, w_in, gmlp_v_norm_g, gmlp_w_s, gmlp_b_s, q_norm_g, k_norm_g,
           group_norm_a_g, group_norm_b_g, w_out, norm_ffn_g, w_router, w_gate, w_up,
           w_down, final_norm_g):
    cos, sin_signed = _rope_tables()
    g256 = jnp.asarray(_group_ones(MXU_DIM), BF16)
    g128 = jnp.asarray(_group_ones(LANES), BF16)

    x2 = x.reshape(N_TOK, D_MODEL)
    w_s = gmlp_w_s[0].astype(BF16)
    wpair = jnp.concatenate([w_s[0::2], w_s[1::2]], axis=2)
    bfull = jnp.repeat(gmlp_b_s[0].T, HEAD_DIM, axis=1)
    an, q, k, v = _inproj(
        x2, norm_mix_g[0][None, :], w_in[0].astype(BF16),
        gmlp_v_norm_g[0].reshape(1, A_WIDTH), wpair, bfull,
        jnp.tile(q_norm_g[0], B_HEADS)[None, :], jnp.tile(k_norm_g[0], B_KV_HEADS)[None, :],
        group_norm_a_g[0][None, :], jnp.asarray(cos), jnp.asarray(sin_signed), g256, g128)

    wo = w_out[0].astype(BF16)
    wr = w_router[0]
    wr_hi = wr.astype(BF16)
    wr_lo = (wr - wr_hi.astype(F32)).astype(BF16)
    o = _attn(q, k, v)
    x1, h2p, aff = _outproj(o, an, x2, wo[:A_WIDTH], wo[A_WIDTH:],
                            group_norm_b_g[0][None, :], norm_ffn_g[0][None, :], wr_hi, wr_lo)

    aff_t = aff.reshape(BATCH, SEQ, N_EXPERTS).transpose(0, 2, 1)
    idx, gates = _topk(aff_t)
    idx_eb = idx.transpose(1, 0, 2).reshape(FFN_ITEMS, FFN_B, CAP)
    idx_be = idx.reshape(BATCH * N_EXPERTS // SC_E, SC_E, CAP)
    g_be = gates.reshape(BATCH * N_EXPERTS // SC_E, SC_E, CAP)

    y = _ffn(idx_eb, h2p.reshape(BATCH, SEQ, HALF), w_gate[0], w_up[0], w_down[0])
    return _scatter(idx_be, g_be, y, x1.reshape(BATCH, SEQ, D_MODEL), final_norm_g[None, :])
```

```python
import numpy as np
import jax
import jax.numpy as jnp
from jax import lax
from jax.experimental import pallas as pl
from jax.experimental.pallas import tpu as pltpu

D_MODEL = 1024
BATCH = 16
SEQ = 2048
HEAD_DIM = 64
A_HEADS = 8
A_WIDTH = A_HEADS * HEAD_DIM
CHUNK = 128
B_HEADS = 8
B_KV_HEADS = 2
B_GROUP = B_HEADS // B_KV_HEADS
B_WIDTH = B_HEADS * HEAD_DIM
KV_WIDTH = B_KV_HEADS * HEAD_DIM
IN_WIDTH = 2 * A_WIDTH + B_WIDTH + 2 * KV_WIDTH
GRID_W = 64
ROPE_BASE = 10000.0
N_EXPERTS = 16
CAP = 2 * SEQ // N_EXPERTS
EXPERT_FF = 1024
EPS = 1e-6
N_TOK = BATCH * SEQ

LANES = 128
MXU_DIM = 256
VMEM_LIMIT = 56 * 1024 * 1024

TM_IN = 1024
IN_SUB = 256
TQ = 1024
OUT_SUB = 256
FFN_B = 2
FFN_ROWS = FFN_B * CAP
SC_E = 4
SC_ROWS = 4
HALF = D_MODEL // 2

F32 = jnp.float32
BF16 = jnp.bfloat16
U32 = jnp.uint32
HI_MASK = np.uint32(0xFFFF0000)


def _dot(a, b):
    return jnp.dot(a, b, preferred_element_type=F32)


def _dot_nt(a, b):
    return lax.dot_general(a, b, (((1,), (1,)), ((), ())), preferred_element_type=F32)


def _gelu(x):
    c = np.float32(np.sqrt(2.0 / np.pi))
    return x * (0.5 * (1.0 + jnp.tanh(c * (x + 0.044715 * (x * x * x)))))


def _group_meansq(x, gmat):
    wc = gmat.shape[0]
    xx = (x * x).astype(BF16)
    parts = [_dot(xx[:, j * wc:(j + 1) * wc], gmat) for j in range(x.shape[1] // wc)]
    ss = parts[0] if len(parts) == 1 else jnp.concatenate(parts, axis=1)
    return ss * (1.0 / HEAD_DIM)


def _rope(xc, cos, sin_signed, first_half):
    partner = jnp.where(first_half, pltpu.roll(xc, LANES - 16, 1), pltpu.roll(xc, 16, 1))
    return xc * cos + partner * sin_signed


def _pack_bf16_pairs(x):
    w = x.shape[1] // 2
    bits = pltpu.bitcast(x.astype(BF16).astype(F32), U32)
    return (bits[:, w:] & HI_MASK) | (bits[:, :w] >> 16)


def _unpack_lo(p):
    return pltpu.bitcast(p << 16, F32)


def _unpack_hi(p):
    return pltpu.bitcast(p & HI_MASK, F32)


def _inproj_kernel(x_ref, gmix_ref, win_ref, gv_ref, wpair_ref, bfull_ref, gq_ref, gk_ref,
                   ga_ref, cos_ref, sin_ref, g256_ref, g128_ref,
                   an_ref, q_ref, k_ref, v_ref):
    for st in range(TM_IN // IN_SUB):
        _inproj_rows(slice(st * IN_SUB, (st + 1) * IN_SUB), x_ref, gmix_ref, win_ref, gv_ref,
                     wpair_ref, bfull_ref, gq_ref, gk_ref, ga_ref, cos_ref, sin_ref, g256_ref,
                     g128_ref, an_ref, q_ref, k_ref, v_ref)


def _inproj_rows(rs, x_ref, gmix_ref, win_ref, gv_ref, wpair_ref, bfull_ref, gq_ref, gk_ref,
                 ga_ref, cos_ref, sin_ref, g256_ref, g128_ref, an_ref, q_ref, k_ref, v_ref):
    x = x_ref[rs, :]
    ms = jnp.mean(x * x, axis=-1, keepdims=True)
    h = (x * lax.rsqrt(ms + EPS) * gmix_ref[...]).astype(BF16)
    p = _dot(h, win_ref[...])

    u = _gelu(p[:, 0:A_WIDTH])
    vg = _gelu(p[:, A_WIDTH:2 * A_WIDTH])
    vn = vg * lax.rsqrt(_group_meansq(vg, g256_ref[...]) + EPS) * gv_ref[...]

    lane = lax.broadcasted_iota(jnp.int32, (CHUNK, LANES), 1)
    left = lane < HEAD_DIM
    rows = []
    for ci in range(IN_SUB // CHUNK):
        r0 = ci * CHUNK
        cols = []
        for j in range(A_WIDTH // LANES):
            vj = vn[r0:r0 + CHUNK, j * LANES:(j + 1) * LANES].astype(BF16)
            zero = jnp.zeros_like(vj)
            rhs = jnp.concatenate([jnp.where(left, vj, zero), jnp.where(left, zero, vj)], axis=0)
            gate = _dot(wpair_ref[j], rhs) + bfull_ref[:, j * LANES:(j + 1) * LANES]
            cols.append(u[r0:r0 + CHUNK, j * LANES:(j + 1) * LANES] * gate)
        rows.append(jnp.concatenate(cols, axis=1))
    a = jnp.concatenate(rows, axis=0)
    a_ms = jnp.mean(a * a, axis=-1, keepdims=True)
    an_ref[rs, :] = (a * lax.rsqrt(a_ms + EPS) * ga_ref[...]).astype(BF16)

    cos = cos_ref[rs, :]
    sin_signed = sin_ref[rs, :]
    lane_t = lax.broadcasted_iota(jnp.int32, (IN_SUB, LANES), 1)
    first_half = (lane_t % 32) < 16
    left_t = lane_t < HEAD_DIM

    o_q = 2 * A_WIDTH
    q = p[:, o_q:o_q + B_WIDTH]
    qn = q * lax.rsqrt(_group_meansq(q, g256_ref[...]) + EPS) * gq_ref[...]
    scale = np.float32(HEAD_DIM ** -0.5 * np.log2(np.e))
    for j in range(B_WIDTH // LANES):
        qj = _rope(qn[:, j * LANES:(j + 1) * LANES], cos, sin_signed, first_half) * scale
        q_ref[0, 2 * j, rs, :] = jnp.where(left_t, qj, 0.0).astype(BF16)
        q_ref[0, 2 * j + 1, rs, :] = jnp.where(left_t, 0.0, qj).astype(BF16)

    o_k = o_q + B_WIDTH
    k = p[:, o_k:o_k + KV_WIDTH]
    kn = k * lax.rsqrt(_group_meansq(k, g128_ref[...]) + EPS) * gk_ref[...]
    kr = _rope(kn, cos, sin_signed, first_half)
    kr_sw = pltpu.roll(kr, HEAD_DIM, 1)
    k_ref[0, 0, rs, :] = jnp.where(left_t, kr, kr_sw).astype(BF16)
    k_ref[0, 1, rs, :] = jnp.where(left_t, kr_sw, kr).astype(BF16)
    v = p[:, o_k + KV_WIDTH:o_k + 2 * KV_WIDTH]
    v_sw = pltpu.roll(v, HEAD_DIM, 1)
    v_ref[0, 0, rs, :] = jnp.where(left_t, v, v_sw).astype(BF16)
    v_ref[0, 1, rs, :] = jnp.where(left_t, v_sw, v).astype(BF16)


def _inproj(x2, gmix, win, gv, wpair, bfull, gq, gk, ga, cos, sin_signed, g256, g128):
    n_steps = N_TOK // TM_IN
    per_seq = SEQ // TM_IN
    full = lambda shape: pl.BlockSpec(shape, lambda i: (0,) * len(shape))
    return pl.pallas_call(
        _inproj_kernel,
        grid=(n_steps,),
        in_specs=[
            pl.BlockSpec((TM_IN, D_MODEL), lambda i: (i, 0)),
            full((1, D_MODEL)),
            full((D_MODEL, IN_WIDTH)),
            full((1, A_WIDTH)),
            full((A_WIDTH // LANES, CHUNK, 2 * CHUNK)),
            full((CHUNK, A_WIDTH)),
            full((1, B_WIDTH)),
            full((1, KV_WIDTH)),
            full((1, A_WIDTH)),
            pl.BlockSpec((TM_IN, LANES), lambda i: (i % per_seq, 0)),
            pl.BlockSpec((TM_IN, LANES), lambda i: (i % per_seq, 0)),
            full((MXU_DIM, MXU_DIM)),
            full((LANES, LANES)),
        ],
        out_specs=[
            pl.BlockSpec((TM_IN, A_WIDTH), lambda i: (i, 0)),
            pl.BlockSpec((1, B_HEADS, TM_IN, LANES), lambda i: (i // per_seq, 0, i % per_seq, 0)),
            pl.BlockSpec((1, B_KV_HEADS, TM_IN, LANES),
                         lambda i: (i // per_seq, 0, i % per_seq, 0)),
            pl.BlockSpec((1, B_KV_HEADS, TM_IN, LANES),
                         lambda i: (i // per_seq, 0, i % per_seq, 0)),
        ],
        out_shape=[
            jax.ShapeDtypeStruct((N_TOK, A_WIDTH), BF16),
            jax.ShapeDtypeStruct((BATCH, B_HEADS, SEQ, LANES), BF16),
            jax.ShapeDtypeStruct((BATCH, B_KV_HEADS, SEQ, LANES), BF16),
            jax.ShapeDtypeStruct((BATCH, B_KV_HEADS, SEQ, LANES), BF16),
        ],
        compiler_params=pltpu.CompilerParams(
            dimension_semantics=("arbitrary",), vmem_limit_bytes=VMEM_LIMIT),
        name="inproj",
    )(x2, gmix, win, gv, wpair, bfull, gq, gk, ga, cos, sin_signed, g256, g128)


ATT_ROWS = 512
ATT_M = 2 * ATT_ROWS
ATT_PAIRS = B_HEADS // 2
ATT_UNITS = (SEQ // ATT_ROWS) * ATT_PAIRS
ATT_KT = MXU_DIM


def _att_unit(u):
    rb = u // ATT_PAIRS
    pair = u % ATT_PAIRS
    r0 = rb * ATT_ROWS
    if not isinstance(u, int):
        r0 = pl.multiple_of(r0, ATT_ROWS)
    return r0, pair, pair // (B_GROUP // 2)


def _att_step(ua, uc, q_ref, k_ref, v_ref, o_ref, s_w, m_w, s_r, m_r, p_w, p_r,
              scores=True, exp=True, pv=True):
    if scores:
        r0a, pair_a, g_a = _att_unit(ua)
        lhs = jnp.concatenate([q_ref[0, 2 * pair_a, pl.ds(r0a, ATT_ROWS), :],
                               q_ref[0, 2 * pair_a + 1, pl.ds(r0a, ATT_ROWS), :]], axis=0)
    if exp:
        m_prev = m_r[...]
    if pv:
        r0c, pair_c, g_c = _att_unit(uc)
        ones = jnp.ones((ATT_KT, LANES), BF16)
    m_run = None
    acc = None
    for j in range(SEQ // ATT_KT):
        ks = slice(j * ATT_KT, (j + 1) * ATT_KT)
        if scores:
            s = _dot_nt(lhs, k_ref[0, g_a, ks, :])
            s_w[:, ks] = s
            for h in range(ATT_KT // LANES):
                sh = s[:, h * LANES:(h + 1) * LANES]
                m_run = sh if m_run is None else jnp.maximum(m_run, sh)
        if exp:
            for h in range(ATT_KT // LANES):
                sl = slice(j * ATT_KT + h * LANES, j * ATT_KT + (h + 1) * LANES)
                p_w[:, sl] = jnp.exp2(s_r[:, sl] - m_prev).astype(BF16)
        if pv:
            rhs = jnp.concatenate([v_ref[0, g_c, ks, :], ones], axis=1)
            part = _dot(p_r[:, ks], rhs)
            acc = part if acc is None else acc + part
    if scores:
        m_w[...] = jnp.broadcast_to(jnp.max(m_run, axis=-1, keepdims=True), (ATT_M, LANES))
    if pv:
        on = acc[:, :LANES] * (1.0 / acc[:, LANES:])
        left = lax.broadcasted_iota(jnp.int32, (ATT_ROWS, LANES), 1) < HEAD_DIM
        o_ref[0, pair_c, pl.ds(r0c, ATT_ROWS), :] = jnp.where(
            left, on[:ATT_ROWS], on[ATT_ROWS:])


def _attn_kernel(q_ref, k_ref, v_ref, o_ref, s0, s1, m0, m1, p0, p1):
    refs = (q_ref, k_ref, v_ref, o_ref)
    last = ATT_UNITS - 1
    _att_step(0, None, *refs, s0, m0, None, None, None, None, exp=False, pv=False)
    _att_step(1, None, *refs, s1, m1, s0, m0, p0, None, pv=False)

    def body(i, carry):
        it = 2 * i + 2
        _att_step(it, it - 2, *refs, s0, m0, s1, m1, p1, p0)

        @pl.when(i >= 0)
        def _():
            _att_step(it + 1, it - 1, *refs, s1, m1, s0, m0, p0, p1)

        return carry

    lax.fori_loop(0, (ATT_UNITS - 2) // 2, body, 0)
    _att_step(None, last - 1, *refs, None, None, s1, m1, p1, p0, scores=False)
    _att_step(None, last, *refs, None, None, None, None, None, p1, scores=False, exp=False)


def _attn(q, k, v):
    return pl.pallas_call(
        _attn_kernel,
        grid=(BATCH,),
        in_specs=[
            pl.BlockSpec((1, B_HEADS, SEQ, LANES), lambda b: (b, 0, 0, 0)),
            pl.BlockSpec((1, B_KV_HEADS, SEQ, LANES), lambda b: (b, 0, 0, 0)),
            pl.BlockSpec((1, B_KV_HEADS, SEQ, LANES), lambda b: (b, 0, 0, 0)),
        ],
        out_specs=pl.BlockSpec((1, ATT_PAIRS, SEQ, LANES), lambda b: (b, 0, 0, 0)),
        out_shape=jax.ShapeDtypeStruct((BATCH, ATT_PAIRS, SEQ, LANES), F32),
        scratch_shapes=[pltpu.VMEM((ATT_M, SEQ), F32), pltpu.VMEM((ATT_M, SEQ), F32),
                        pltpu.VMEM((ATT_M, LANES), F32), pltpu.VMEM((ATT_M, LANES), F32),
                        pltpu.VMEM((ATT_M, SEQ), BF16), pltpu.VMEM((ATT_M, SEQ), BF16)],
        compiler_params=pltpu.CompilerParams(
            dimension_semantics=("arbitrary",), vmem_limit_bytes=VMEM_LIMIT),
        name="attn",
    )(q, k, v)


def _outproj_kernel(o_ref, an_ref, x_ref, woa_ref, wob_ref, gb_ref, gffn_ref, wrh_ref, wrc_ref,
                    x1_ref, h2p_ref, aff_ref):
    for st in range(TQ // OUT_SUB):
        rs = slice(st * OUT_SUB, (st + 1) * OUT_SUB)
        o = jnp.concatenate([o_ref[0, j, rs, :] for j in range(ATT_PAIRS)], axis=1)
        o_ms = jnp.mean(o * o, axis=-1, keepdims=True)
        bn = (o * lax.rsqrt(o_ms + EPS) * gb_ref[...]).astype(BF16)
        x1 = x_ref[rs, :] + _dot(an_ref[rs, :], woa_ref[...]) + _dot(bn, wob_ref[...])
        x1_ref[rs, :] = x1
        ms = jnp.mean(x1 * x1, axis=-1, keepdims=True)
        h2 = x1 * lax.rsqrt(ms + EPS) * gffn_ref[...]
        h2p_ref[rs, :] = _pack_bf16_pairs(h2)
        h2_hi = h2.astype(BF16)
        h2_lo = (h2 - h2_hi.astype(F32)).astype(BF16)
        hw = _dot(h2_hi, wrc_ref[...])
        logits = hw[:, :N_EXPERTS] + hw[:, N_EXPERTS:] + _dot(h2_lo, wrh_ref[...])
        lm = jnp.max(logits, axis=-1, keepdims=True)
        le = jnp.exp(logits - lm)
        aff_ref[rs, :] = le / jnp.sum(le, axis=-1, keepdims=True)


def _outproj(o, an, x2, woa, wob, gb, gffn, wrh, wrc):
    nq = SEQ // TQ
    full = lambda shape: pl.BlockSpec(shape, lambda b, i: (0,) * len(shape))
    tok = lambda w: pl.BlockSpec((TQ, w), lambda b, i: (b * nq + i, 0))
    return pl.pallas_call(
        _outproj_kernel,
        grid=(BATCH, nq),
        in_specs=[
            pl.BlockSpec((1, ATT_PAIRS, TQ, LANES), lambda b, i: (b, 0, i, 0)),
            tok(A_WIDTH),
            tok(D_MODEL),
            full((A_WIDTH, D_MODEL)),
            full((B_WIDTH, D_MODEL)),
            full((1, B_WIDTH)),
            full((1, D_MODEL)),
            full((D_MODEL, N_EXPERTS)),
            full((D_MODEL, 2 * N_EXPERTS)),
        ],
        out_specs=[tok(D_MODEL), tok(HALF), tok(N_EXPERTS)],
        out_shape=[
            jax.ShapeDtypeStruct((N_TOK, D_MODEL), F32),
            jax.ShapeDtypeStruct((N_TOK, HALF), U32),
            jax.ShapeDtypeStruct((N_TOK, N_EXPERTS), F32),
        ],
        compiler_params=pltpu.CompilerParams(
            dimension_semantics=("arbitrary", "arbitrary"), vmem_limit_bytes=VMEM_LIMIT),
        name="outproj",
    )(o, an, x2, woa, wob, gb, gffn, wrh, wrc)


def _prefix_lanes(m, tri):
    outs = []
    run = jnp.zeros((m.shape[0], 1), F32)
    for t in range(SEQ // LANES):
        pt = _dot(m[:, t * LANES:(t + 1) * LANES].astype(BF16), tri) + run
        outs.append(pt)
        run = pt[:, LANES - 1:LANES]
    return jnp.concatenate(outs, axis=1)


def _topk_kernel(aff_ref, idx_ref, g_ref, psel_scr, hi_scr, mid_scr, lo_scr):
    a = aff_ref[0]
    key = pltpu.bitcast(a, jnp.int32)

    def bit_step(i, t):
        cand = t | jnp.left_shift(jnp.int32(1), 30 - i)
        cnt = jnp.sum(jnp.where(key >= cand, 1.0, 0.0), axis=1, keepdims=True)
        return jnp.where(cnt >= CAP, cand, t)

    t = lax.fori_loop(0, 31, bit_step, jnp.zeros((N_EXPERTS, 1), jnp.int32))
    gt = jnp.where(key > t, 1.0, 0.0)
    eq = jnp.where(key == t, 1.0, 0.0)
    need = CAP - jnp.sum(gt, axis=1, keepdims=True)

    r = lax.broadcasted_iota(jnp.int32, (LANES, LANES), 0)
    c = lax.broadcasted_iota(jnp.int32, (LANES, LANES), 1)
    tri = jnp.where(r <= c, 1.0, 0.0).astype(BF16)
    sel = gt + eq * jnp.where(_prefix_lanes(eq, tri) <= need, 1.0, 0.0)
    psel_scr[...] = sel * _prefix_lanes(sel, tri)

    hi = a.astype(BF16)
    r1 = a - hi.astype(F32)
    mid = r1.astype(BF16)
    lo = (r1 - mid.astype(F32)).astype(BF16)
    hi_scr[...] = hi.astype(F32)
    mid_scr[...] = mid.astype(F32)
    lo_scr[...] = lo.astype(F32)

    pos = lax.broadcasted_iota(jnp.int32, (8, SEQ), 1)
    row = lax.broadcasted_iota(jnp.int32, (8, SEQ), 0)
    pos_hi = (pos >> 7).astype(F32)
    pos_lo = (pos & 127).astype(F32)
    slot = (lax.broadcasted_iota(jnp.int32, (CAP, 1), 0) + 1).astype(F32)

    def expert_step(e, carry):
        prow = psel_scr[pl.ds(e, 1), :]
        onehot = jnp.where(prow == slot, 1.0, 0.0).astype(BF16)
        lhs = jnp.where(row == 0, hi_scr[pl.ds(e, 1), :],
              jnp.where(row == 1, mid_scr[pl.ds(e, 1), :],
              jnp.where(row == 2, lo_scr[pl.ds(e, 1), :],
              jnp.where(row == 3, pos_hi,
              jnp.where(row == 4, pos_lo, 0.0))))).astype(BF16)
        res = _dot_nt(lhs, onehot)
        g_ref[0, pl.ds(e, 1), :] = res[0:1] + res[1:2] + res[2:3]
        idx_ref[0, pl.ds(e, 1), :] = (res[3:4] * 128.0 + res[4:5]).astype(jnp.int32)
        return carry

    lax.fori_loop(0, N_EXPERTS, expert_step, 0)


def _topk(aff_t):
    return pl.pallas_call(
        _topk_kernel,
        grid=(BATCH,),
        in_specs=[pl.BlockSpec((1, N_EXPERTS, SEQ), lambda b: (b, 0, 0))],
        out_specs=[pl.BlockSpec((1, N_EXPERTS, CAP), lambda b: (b, 0, 0)),
                   pl.BlockSpec((1, N_EXPERTS, CAP), lambda b: (b, 0, 0))],
        out_shape=[jax.ShapeDtypeStruct((BATCH, N_EXPERTS, CAP), jnp.int32),
                   jax.ShapeDtypeStruct((BATCH, N_EXPERTS, CAP), F32)],
        scratch_shapes=[pltpu.VMEM((N_EXPERTS, SEQ), F32)] * 4,
        compiler_params=pltpu.CompilerParams(
            dimension_semantics=("arbitrary",), vmem_limit_bytes=VMEM_LIMIT),
        name="topk",
    )(aff_t)


FFN_NB = BATCH // FFN_B
FFN_ITEMS = N_EXPERTS * FFN_NB


def _ffn_kernel(idx_ref, h_ref, wg_ref, wu_ref, wd_ref, y_ref, wg_s, wu_s, wd_s, xs_scr):
    t = pl.program_id(0)
    item = jnp.maximum(t - 1, 0)

    @pl.when(t == 0)
    def _():
        xs_scr[...] = jnp.zeros_like(xs_scr)

    @pl.when(item % FFN_NB == 0)
    def _():
        wg_s[...] = wg_ref[0].astype(BF16)
        wu_s[...] = wu_ref[0].astype(BF16)
        wd_s[...] = wd_ref[0].astype(BF16)

    xs = xs_scr[(t + 1) % 2]
    xs_lo = _unpack_lo(xs).astype(BF16)
    xs_hi = _unpack_hi(xs).astype(BF16)
    a = _dot(xs_lo, wg_s[0:HALF, :]) + _dot(xs_hi, wg_s[HALF:, :])
    b = _dot(xs_lo, wu_s[0:HALF, :]) + _dot(xs_hi, wu_s[HALF:, :])
    hm = (a / (1.0 + jnp.exp(-a)) * b).astype(BF16)
    y_ref[0] = _dot(hm, wd_s[...])

    slot = t % 2
    for bb in range(FFN_B):
        for c in range(CAP):
            i = idx_ref[0, bb, c]
            xs_scr[slot, pl.ds(bb * CAP + c, 1), :] = h_ref[bb, pl.ds(i, 1), :]


def _ffn(idx_eb, h2p, w_gate, w_up, w_down):
    item_of = lambda t: jnp.maximum(t - 1, 0)
    wspec = lambda k, n: pl.BlockSpec((1, k, n), lambda t: (item_of(t) // FFN_NB, 0, 0))
    return pl.pallas_call(
        _ffn_kernel,
        grid=(FFN_ITEMS + 1,),
        in_specs=[
            pl.BlockSpec((1, FFN_B, CAP), lambda t: (jnp.minimum(t, FFN_ITEMS - 1), 0, 0),
                         memory_space=pltpu.SMEM),
            pl.BlockSpec((FFN_B, SEQ, HALF), lambda t: (t % FFN_NB, 0, 0)),
            wspec(D_MODEL, EXPERT_FF), wspec(D_MODEL, EXPERT_FF), wspec(EXPERT_FF, D_MODEL),
        ],
        out_specs=pl.BlockSpec(
            (1, FFN_ROWS, D_MODEL), lambda t: (item_of(t) // FFN_NB, item_of(t) % FFN_NB, 0)),
        out_shape=jax.ShapeDtypeStruct((N_EXPERTS, BATCH * CAP, D_MODEL), F32),
        scratch_shapes=[pltpu.VMEM((D_MODEL, EXPERT_FF), BF16),
                        pltpu.VMEM((D_MODEL, EXPERT_FF), BF16),
                        pltpu.VMEM((EXPERT_FF, D_MODEL), BF16),
                        pltpu.VMEM((2, FFN_ROWS, HALF), U32)],
        compiler_params=pltpu.CompilerParams(
            dimension_semantics=("arbitrary",), vmem_limit_bytes=VMEM_LIMIT),
        name="ffn",
    )(idx_eb, h2p, w_gate, w_up, w_down)


NORM_ROWS = 256


def _scatter_kernel(idx_ref, g_ref, y_ref, x1_ref, gfin_ref, o_ref):
    ec = pl.program_id(1)

    @pl.when(ec == 0)
    def _():
        o_ref[...] = x1_ref[...]

    for el in range(SC_E):
        def load_group(c0):
            rows = [idx_ref[0, el, c0 + r] for r in range(SC_ROWS)]
            return rows, [o_ref[0, pl.ds(rows[r], 1), :] for r in range(SC_ROWS)]

        rows, cur = load_group(0)
        for c0 in range(0, CAP, SC_ROWS):
            nxt = load_group(c0 + SC_ROWS) if c0 + SC_ROWS < CAP else None
            for r in range(SC_ROWS):
                gate = g_ref[0, el, c0 + r]
                o_ref[0, pl.ds(rows[r], 1), :] = cur[r] + gate * y_ref[el, c0 + r:c0 + r + 1, :]
            if nxt is not None:
                rows, cur = nxt

    @pl.when(ec == N_EXPERTS // SC_E - 1)
    def _():
        def norm_step(r, carry):
            r0 = pl.multiple_of(r * NORM_ROWS, NORM_ROWS)
            xx = o_ref[0, pl.ds(r0, NORM_ROWS), :]
            ms = jnp.mean(xx * xx, axis=-1, keepdims=True)
            o_ref[0, pl.ds(r0, NORM_ROWS), :] = xx * lax.rsqrt(ms + EPS) * gfin_ref[...]
            return carry

        lax.fori_loop(0, SEQ // NORM_ROWS, norm_step, 0)


def _scatter(idx_be, g_be, y, x1, gfin):
    ne = N_EXPERTS // SC_E
    smem = lambda: pl.BlockSpec((1, SC_E, CAP), lambda b, ec: (b * ne + ec, 0, 0),
                                memory_space=pltpu.SMEM)
    return pl.pallas_call(
        _scatter_kernel,
        grid=(BATCH, ne),
        in_specs=[
            smem(), smem(),
            pl.BlockSpec((SC_E, CAP, D_MODEL), lambda b, ec: (ec, b, 0)),
            pl.BlockSpec((1, SEQ, D_MODEL), lambda b, ec: (b, 0, 0)),
            pl.BlockSpec((1, D_MODEL), lambda b, ec: (0, 0)),
        ],
        out_specs=pl.BlockSpec((1, SEQ, D_MODEL), lambda b, ec: (b, 0, 0)),
        out_shape=jax.ShapeDtypeStruct((BATCH, SEQ, D_MODEL), F32),
        compiler_params=pltpu.CompilerParams(
            dimension_semantics=("arbitrary", "arbitrary"), vmem_limit_bytes=VMEM_LIMIT),
        name="scatter",
    )(idx_be, g_be, y, x1, gfin)


def _rope_tables():
    half = HEAD_DIM // 4
    inv_freq = (np.float32(ROPE_BASE) ** (-np.arange(half, dtype=np.float32) / np.float32(half)))
    t = np.arange(SEQ)
    row_ang = (t // GRID_W).astype(np.float32)[:, None] * inv_freq[None, :]
    col_ang = (t % GRID_W).astype(np.float32)[:, None] * inv_freq[None, :]
    ang = np.concatenate([row_ang, row_ang, col_ang, col_ang], axis=1).astype(np.float64)
    sign = np.concatenate([-np.ones(half), np.ones(half), -np.ones(half), np.ones(half)])
    cos = np.tile(np.cos(ang), (1, LANES // HEAD_DIM)).astype(np.float32)
    sin_signed = np.tile(np.sin(ang) * sign[None, :], (1, LANES // HEAD_DIM)).astype(np.float32)
    return cos, sin_signed


def _group_ones(width):
    g = np.arange(width) // HEAD_DIM
    return (g[:, None] == g[None, :]).astype(np.float32)


def kernel(x, norm_mix_g, w_in, gmlp_v_norm_g, gmlp_w_s, gmlp_b_s, q_norm_g, k_norm_g,
           group_norm_a_g, group_norm_b_g, w_out, norm_ffn_g, w_router, w_gate, w_up,
           w_down, final_norm_g):
    cos, sin_signed = _rope_tables()
    g256 = jnp.asarray(_group_ones(MXU_DIM), BF16)
    g128 = jnp.asarray(_group_ones(LANES), BF16)

    x2 = x.reshape(N_TOK, D_MODEL)
    w_s = gmlp_w_s[0].astype(BF16)
    wpair = jnp.concatenate([w_s[0::2], w_s[1::2]], axis=2)
    bfull = jnp.repeat(gmlp_b_s[0].T, HEAD_DIM, axis=1)
    an, q, k, v = _inproj(
        x2, norm_mix_g[0][None, :], w_in[0].astype(BF16),
        gmlp_v_norm_g[0].reshape(1, A_WIDTH), wpair, bfull,
        jnp.tile(q_norm_g[0], B_HEADS)[None, :], jnp.tile(k_norm_g[0], B_KV_HEADS)[None, :],
        group_norm_a_g[0][None, :], jnp.asarray(cos), jnp.asarray(sin_signed), g256, g128)

    wo = w_out[0].astype(BF16)
    wr = w_router[0]
    wr_hi = wr.astype(BF16)
    wr_lo = (wr - wr_hi.astype(F32)).astype(BF16)
    o = _attn(q, k, v)
    x1, h2p, aff = _outproj(o, an, x2, wo[:A_WIDTH], wo[A_WIDTH:],
                            group_norm_b_g[0][None, :], norm_ffn_g[0][None, :], wr_hi,
                            jnp.concatenate([wr_hi, wr_lo], axis=1))

    aff_t = aff.reshape(BATCH, SEQ, N_EXPERTS).transpose(0, 2, 1)
    idx, gates = _topk(aff_t)
    idx_eb = idx.transpose(1, 0, 2).reshape(FFN_ITEMS, FFN_B, CAP)
    idx_be = idx.reshape(BATCH * N_EXPERTS // SC_E, SC_E, CAP)
    g_be = gates.reshape(BATCH * N_EXPERTS // SC_E, SC_E, CAP)

    y = _ffn(idx_eb, h2p.reshape(BATCH, SEQ, HALF), w_gate[0], w_up[0], w_down[0])
    return _scatter(idx_be, g_be, y, x1.reshape(BATCH, SEQ, D_MODEL), final_norm_g[None, :])
```
